```python
import math
import jax
import jax.numpy as jnp
from jax import lax
import numpy as np

D_MODEL = 1024
BATCH = 8
SEQ = 2048
DEPTH = 2

CTX_LEN = 256
GRID_W = 64
N_BRANCH = 4
HEAD_DIM = 64
W_A = 512
CONV_K = 31
H_B = 8
W_B = H_B * HEAD_DIM
NA_ROWS = 8
NA_COLS = 16
H_C = 8
KV_C = 2
W_C = H_C * HEAD_DIM
W_C_KV = KV_C * HEAD_DIM
H_D = 4
W_D = H_D * 2 * HEAD_DIM
Q_BLOCK = 128
ROPE_THETA = 10000.0
EPS = 1e-6
NEG_INF = -1e30
IN_SIZES = (2 * W_A, W_A, 3 * W_B, W_B, W_C + 2 * W_C_KV, W_C, 3 * W_D, W_D, N_BRANCH * D_MODEL)
IN_W = 2 * W_A + W_A + 3 * W_B + W_B + W_C + 2 * W_C_KV + W_C + 3 * W_D + W_D + N_BRANCH * D_MODEL

kernel_name = 'hybrid_parallel_gated_dit_block'


def rms_norm(x, g):
    xf = x.astype(jnp.float32)
    y = xf * lax.rsqrt(jnp.mean(xf * xf, axis=-1, keepdims=True) + EPS)
    return (y * g.astype(jnp.float32)).astype(x.dtype)


def layer_norm(x, g, b):
    xf = x.astype(jnp.float32)
    mu = jnp.mean(xf, axis=-1, keepdims=True)
    var = jnp.mean(jnp.square(xf - mu), axis=-1, keepdims=True)
    y = (xf - mu) * lax.rsqrt(var + EPS)
    return (y * g.astype(jnp.float32) + b.astype(jnp.float32)).astype(x.dtype)


def grid_positions(t_len):
    t = jnp.arange(t_len)
    return (t // GRID_W).astype(jnp.float32), (t % GRID_W).astype(jnp.float32)


def rope_1d(x, pos):
    dr = x.shape[-1]
    freqs = ROPE_THETA ** (-jnp.arange(0, dr, 2, dtype=jnp.float32) / dr)
    ang = pos[:, None] * freqs[None, :]
    cos = jnp.cos(ang)[:, None, :].astype(x.dtype)
    sin = jnp.sin(ang)[:, None, :].astype(x.dtype)
    x1, x2 = jnp.split(x, 2, axis=-1)
    return jnp.concatenate([x1 * cos - x2 * sin, x2 * cos + x1 * sin], axis=-1)


def rope_2d(x, rows, cols):
    half = x.shape[-1] // 2
    return jnp.concatenate([rope_1d(x[..., :half], rows), rope_1d(x[..., half:], cols)], axis=-1)


def split_cols(p):
    return jnp.split(p, np.cumsum(IN_SIZES)[:-1].tolist(), axis=-1)


def blockwise(fn, q):
    b_, t = q.shape[:2]
    nb = t // Q_BLOCK
    qb = jnp.moveaxis(q.reshape(b_, nb, Q_BLOCK, *q.shape[2:]), 1, 0)
    out = lax.map(fn, qb)
    return jnp.moveaxis(out, 0, 1).reshape(b_, t, *out.shape[3:])


def grouped_attention(q, k, v):
    s = jnp.einsum('bqngd,bsnd->bngqs', q, k).astype(jnp.float32) * (q.shape[-1] ** -0.5)
    p = jax.nn.softmax(s, axis=-1).astype(v.dtype)
    return jnp.einsum('bngqs,bsnd->bqngd', p, v)


def diff_attention(q, k, v, lam):
    s = jnp.einsum('bqhtd,bkhtd->bhtqk', q, k).astype(jnp.float32) * (q.shape[-1] ** -0.5)
    p = jax.nn.softmax(s, axis=-1)
    a = (p[:, :, 0] - lam * p[:, :, 1]).astype(v.dtype)
    return jnp.einsum('bhqk,bkhe->bqhe', a, v)


def conformer_conv(u, conv_w, conv_b, ln_g, ln_b):
    a, g = jnp.split(u, 2, axis=-1)
    h = a * jax.nn.sigmoid(g)
    h = lax.conv_general_dilated(h, conv_w[:, None, :], window_strides=(1,),
                                 padding=[(CONV_K // 2, CONV_K // 2)],
                                 dimension_numbers=('NWC', 'WIO', 'NWC'),
                                 feature_group_count=W_A) + conv_b
    return jax.nn.silu(layer_norm(h, ln_g, ln_b))


def natten(q, k, v, kc, vc, rpb):
    b_, s_len, h_, hd = q.shape
    rows = s_len // GRID_W
    kr = min(NA_ROWS, rows)
    r = jnp.arange(rows)
    r0 = jnp.clip(r - kr // 2, 0, rows - kr)
    row_idx = r0[:, None] + jnp.arange(kr)[None, :]
    cq = jnp.arange(GRID_W)
    c0 = jnp.clip(cq - NA_COLS // 2, 0, GRID_W - NA_COLS)
    col_ok = (cq[None, :] >= c0[:, None]) & (cq[None, :] < c0[:, None] + NA_COLS)
    mask = jnp.broadcast_to(col_ok[:, None, :], (GRID_W, kr, GRID_W)).reshape(GRID_W, kr * GRID_W)
    dr = row_idx - r[:, None] + (NA_ROWS - 1)
    dc = jnp.clip(cq[None, :] - cq[:, None] + (NA_COLS - 1), 0, 2 * NA_COLS - 2)
    bias = rpb[:, dr[:, None, :, None], dc[None, :, None, :]]
    bias = bias.reshape(h_, rows, GRID_W, kr * GRID_W).astype(jnp.float32)
    qg = q.reshape(b_, rows, GRID_W, h_, hd)
    kband = k.reshape(b_, rows, GRID_W, h_, hd)[:, row_idx].reshape(b_, rows, kr * GRID_W, h_, hd)
    vband = v.reshape(b_, rows, GRID_W, h_, hd)[:, row_idx].reshape(b_, rows, kr * GRID_W, h_, hd)
    scale = hd ** -0.5
    s_lat = jnp.einsum('brqhd,brkhd->bhrqk', qg, kband).astype(jnp.float32) * scale + bias
    s_lat = jnp.where(mask, s_lat, NEG_INF)
    s_ctx = jnp.einsum('brqhd,blhd->bhrql', qg, kc).astype(jnp.float32) * scale
    p = jax.nn.softmax(jnp.concatenate([s_lat, s_ctx], axis=-1), axis=-1).astype(v.dtype)
    nk = kr * GRID_W
    o = (jnp.einsum('bhrqk,brkhd->brqhd', p[..., :nk], vband)
         + jnp.einsum('bhrql,blhd->brqhd', p[..., nk:], vc))
    return o.reshape(b_, s_len, h_ * hd)


def merge_branches(ya, yb, yc, yd, logits, b_merge, w_br_a, w_br_b, w_br_c, w_br_d, w_out):
    ga, gb, gc, gd = jnp.split(jax.nn.sigmoid(logits + b_merge), N_BRANCH, axis=-1)
    m = ga * (ya @ w_br_a) + gb * (yb @ w_br_b) + gc * (yc @ w_br_c) + gd * (yd @ w_br_d)
    return m @ w_out


def hybrid_layer(l, x, xc, c, c_ctx, need_ctx, w_ada, b_ada, norm_g, w_in, b_merge,
                 conv_w, conv_b, conv_ln_g, conv_ln_b, na_qn_g, na_kn_g, na_rpb,
                 gqa_qn_g, gqa_kn_g, diff_qn_g, diff_kn_g, lam_q1, lam_k1, lam_q2, lam_k2,
                 diff_subln_g, w_br_a, w_br_b, w_br_c, w_br_d, w_out):
    b_, s_len, _ = x.shape
    l_len = xc.shape[1]
    rows, cols = grid_positions(s_len)
    shift_x, scale_x, gate_x = jnp.split((jax.nn.silu(c) @ w_ada + b_ada)[:, None, :], 3, axis=-1)
    shift_c, scale_c, gate_c = jnp.split(jax.nn.silu(c_ctx) @ w_ada + b_ada, 3, axis=-1)
    h = rms_norm(x, norm_g) * (1.0 + scale_x) + shift_x
    hc = rms_norm(xc, norm_g) * (1.0 + scale_c) + shift_c
    a_in, a_gate, b_qkv, b_gate, c_qkv, c_gate, d_qkv, d_gate, logits = split_cols(h @ w_in)
    ca_in, ca_gate, cb_qkv, cb_gate, cc_qkv, cc_gate, cd_qkv, cd_gate, clogits = split_cols(hc @ w_in)

    y_a = conformer_conv(a_in, conv_w, conv_b, conv_ln_g, conv_ln_b) * jax.nn.silu(a_gate)

    qkv_b = b_qkv.reshape(b_, s_len, 3, H_B, HEAD_DIM)
    cqkv_b = cb_qkv.reshape(b_, l_len, 3, H_B, HEAD_DIM)
    kc_b = rms_norm(cqkv_b[:, :, 1], na_kn_g)
    vc_b = cqkv_b[:, :, 2]
    y_b = natten(rms_norm(qkv_b[:, :, 0], na_qn_g), rms_norm(qkv_b[:, :, 1], na_kn_g), qkv_b[:, :, 2],
                 kc_b, vc_b, na_rpb) * jax.nn.silu(b_gate)

    q_c = rope_2d(rms_norm(c_qkv[..., :W_C].reshape(b_, s_len, H_C, HEAD_DIM), gqa_qn_g), rows, cols)
    k_c = rope_2d(rms_norm(c_qkv[..., W_C:W_C + W_C_KV].reshape(b_, s_len, KV_C, HEAD_DIM), gqa_kn_g), rows, cols)
    v_c = c_qkv[..., W_C + W_C_KV:].reshape(b_, s_len, KV_C, HEAD_DIM)
    kc_c = rms_norm(cc_qkv[..., W_C:W_C + W_C_KV].reshape(b_, l_len, KV_C, HEAD_DIM), gqa_kn_g)
    vc_c = cc_qkv[..., W_C + W_C_KV:].reshape(b_, l_len, KV_C, HEAD_DIM)
    k_all_c = jnp.concatenate([k_c, kc_c], axis=1)
    v_all_c = jnp.concatenate([v_c, vc_c], axis=1)
    q_c = q_c.reshape(b_, s_len, KV_C, H_C // KV_C, HEAD_DIM)
    y_c = blockwise(lambda qi: grouped_attention(qi, k_all_c, v_all_c), q_c).reshape(b_, s_len, W_C)
    y_c = y_c * jax.nn.silu(c_gate)

    lam_init = 0.8 - 0.6 * math.exp(-0.3 * l)
    lam = (jnp.exp(jnp.sum(lam_q1.astype(jnp.float32) * lam_k1.astype(jnp.float32)))
           - jnp.exp(jnp.sum(lam_q2.astype(jnp.float32) * lam_k2.astype(jnp.float32))) + lam_init)
    qkv_d = d_qkv.reshape(b_, s_len, 3, H_D, 2, HEAD_DIM)
    q_d = rope_2d(rms_norm(qkv_d[:, :, 0], diff_qn_g).reshape(b_, s_len, 2 * H_D, HEAD_DIM), rows, cols)
    k_d = rope_2d(rms_norm(qkv_d[:, :, 1], diff_kn_g).reshape(b_, s_len, 2 * H_D, HEAD_DIM), rows, cols)
    q_d = q_d.reshape(b_, s_len, H_D, 2, HEAD_DIM)
    k_d = k_d.reshape(b_, s_len, H_D, 2, HEAD_DIM)
    v_d = qkv_d[:, :, 2].reshape(b_, s_len, H_D, 2 * HEAD_DIM)
    cqkv_d = cd_qkv.reshape(b_, l_len, 3, H_D, 2, HEAD_DIM)
    kc_d = rms_norm(cqkv_d[:, :, 1], diff_kn_g)
    vc_d = cqkv_d[:, :, 2].reshape(b_, l_len, H_D, 2 * HEAD_DIM)
    k_all_d = jnp.concatenate([k_d, kc_d], axis=1)
    v_all_d = jnp.concatenate([v_d, vc_d], axis=1)
    o_d = blockwise(lambda qi: diff_attention(qi, k_all_d, v_all_d, lam), q_d)
    y_d = (rms_norm(o_d, diff_subln_g) * (1.0 - lam_init)).reshape(b_, s_len, W_D) * jax.nn.silu(d_gate)

    x_out = x + gate_x * merge_branches(y_a, y_b, y_c, y_d, logits, b_merge,
                                        w_br_a, w_br_b, w_br_c, w_br_d, w_out)

    if need_ctx:
        yc_a = conformer_conv(ca_in, conv_w, conv_b, conv_ln_g, conv_ln_b) * jax.nn.silu(ca_gate)
        qc_b = rms_norm(cqkv_b[:, :, 0], na_qn_g)
        yc_b = grouped_attention(qc_b[:, :, :, None, :], kc_b, vc_b).reshape(b_, l_len, W_B)
        yc_b = yc_b * jax.nn.silu(cb_gate)
        qc_c = rms_norm(cc_qkv[..., :W_C].reshape(b_, l_len, H_C, HEAD_DIM), gqa_qn_g)
        qc_c = qc_c.reshape(b_, l_len, KV_C, H_C // KV_C, HEAD_DIM)
        yc_c = grouped_attention(qc_c, kc_c, vc_c).reshape(b_, l_len, W_C) * jax.nn.silu(cc_gate)
        qc_d = rms_norm(cqkv_d[:, :, 0], diff_qn_g)
        oc_d = diff_attention(qc_d, kc_d, vc_d, lam)
        yc_d = (rms_norm(oc_d, diff_subln_g) * (1.0 - lam_init)).reshape(b_, l_len, W_D) * jax.nn.silu(cd_gate)
        xc = xc + gate_c * merge_branches(yc_a, yc_b, yc_c, yc_d, clogits, b_merge,
                                          w_br_a, w_br_b, w_br_c, w_br_d, w_out)
    return x_out, xc


def setup_inputs(seed: int = 0) -> dict:
    key = jax.random.key(seed)
    ks = iter(jax.random.split(key, 40))

    def nrm(shape, s):
        return jax.random.normal(next(ks), shape, jnp.float32) * s

    def gain(shape):
        return 1.0 + nrm(shape, 0.02)

    L = DEPTH
    return {
        'x': nrm((BATCH, SEQ, D_MODEL), 1.0),
        'c': nrm((BATCH, D_MODEL), 1.0),
        'ctx': nrm((BATCH, CTX_LEN, D_MODEL), 1.0),
        'c_ctx': nrm((D_MODEL,), 1.0),
        'w_ada': nrm((L, D_MODEL, 3 * D_MODEL), D_MODEL ** -0.5),
        'b_ada': nrm((L, 3 * D_MODEL), 0.02),
        'norm_g': gain((L, D_MODEL)),
        'w_in': nrm((L, D_MODEL, IN_W), D_MODEL ** -0.5),
        'b_merge': nrm((L, N_BRANCH * D_MODEL), 0.02),
        'conv_w': nrm((L, CONV_K, W_A), CONV_K ** -0.5),
        'conv_b': nrm((L, W_A), 0.02),
        'conv_ln_g': gain((L, W_A)),
        'conv_ln_b': nrm((L, W_A), 0.02),
        'na_qn_g': gain((L, HEAD_DIM)),
        'na_kn_g': gain((L, HEAD_DIM)),
        'na_rpb': nrm((L, H_B, 2 * NA_ROWS - 1, 2 * NA_COLS - 1), 0.1),
        'gqa_qn_g': gain((L, HEAD_DIM)),
        'gqa_kn_g': gain((L, HEAD_DIM)),
        'diff_qn_g': gain((L, HEAD_DIM)),
        'diff_kn_g': gain((L, HEAD_DIM)),
        'lam_q1': nrm((L, HEAD_DIM), 0.1),
        'lam_k1': nrm((L, HEAD_DIM), 0.1),
        'lam_q2': nrm((L, HEAD_DIM), 0.1),
        'lam_k2': nrm((L, HEAD_DIM), 0.1),
        'diff_subln_g': gain((L, 2 * HEAD_DIM)),
        'w_br_a': nrm((L, W_A, D_MODEL), W_A ** -0.5),
        'w_br_b': nrm((L, W_B, D_MODEL), W_B ** -0.5),
        'w_br_c': nrm((L, W_C, D_MODEL), W_C ** -0.5),
        'w_br_d': nrm((L, W_D, D_MODEL), W_D ** -0.5),
        'w_out': nrm((L, D_MODEL, D_MODEL), D_MODEL ** -0.5),
    }


def reference(x, c, ctx, c_ctx, w_ada, b_ada, norm_g, w_in, b_merge, conv_w, conv_b, conv_ln_g,
              conv_ln_b, na_qn_g, na_kn_g, na_rpb, gqa_qn_g, gqa_kn_g, diff_qn_g, diff_kn_g,
              lam_q1, lam_k1, lam_q2, lam_k2, diff_subln_g, w_br_a, w_br_b, w_br_c, w_br_d, w_out):
    xc = ctx
    for l in range(DEPTH):
        x, xc = hybrid_layer(l, x, xc, c, c_ctx, l < DEPTH - 1, w_ada[l], b_ada[l], norm_g[l], w_in[l],
                             b_merge[l], conv_w[l], conv_b[l], conv_ln_g[l], conv_ln_b[l], na_qn_g[l],
                             na_kn_g[l], na_rpb[l], gqa_qn_g[l], gqa_kn_g[l], diff_qn_g[l], diff_kn_g[l],
                             lam_q1[l], lam_k1[l], lam_q2[l], lam_k2[l], diff_subln_g[l], w_br_a[l],
                             w_br_b[l], w_br_c[l], w_br_d[l], w_out[l])
    return x
```

```python
import functools
import math

import numpy as np
import jax
import jax.numpy as jnp
from jax import lax
from jax.experimental import pallas as pl
from jax.experimental.pallas import tpu as pltpu

F32 = jnp.float32
BF16 = jnp.bfloat16

GRID_W = 64
HEAD_DIM = 64
N_BRANCH = 4
W_A = 512
CONV_K = 31
CONV_PAD = 16
H_B = 8
W_B = H_B * HEAD_DIM
NA_ROWS = 8
NA_COLS = 16
H_C = 8
KV_C = 2
W_C = H_C * HEAD_DIM
W_C_KV = KV_C * HEAD_DIM
H_D = 4
W_D = H_D * 2 * HEAD_DIM
ROPE_THETA = 10000.0
EPS = 1e-6
NEG_INF = -1e30
LOG2E = math.log2(math.e)
SM_SCALE = HEAD_DIM ** -0.5

OFF_A = 0
OFF_B = OFF_A + 3 * W_A
OFF_C = OFF_B + 4 * W_B
OFF_D = OFF_C + 2 * W_C + 2 * W_C_KV
OFF_L = OFF_D + 4 * W_D
LANES = 128
VMEM_LIMIT = 56 * 1024 * 1024


def _cparams(*sem):
    return pltpu.CompilerParams(dimension_semantics=sem, vmem_limit_bytes=VMEM_LIMIT)


def _const_spec(shape):
    nd = len(shape)
    return pl.BlockSpec(shape, lambda *_: (0,) * nd, pipeline_mode=pl.Buffered(1))


def _sigmoid(x):
    return 1.0 / (1.0 + jnp.exp(-x))


def _silu(x):
    return x * _sigmoid(x)


def _dot(a, b):
    return jnp.dot(a, b, preferred_element_type=F32)


def _dot_nt(a, b):
    return lax.dot_general(a, b, (((1,), (1,)), ((), ())), preferred_element_type=F32)


def _split_bf16(x):
    hi = x.astype(BF16)
    lo = (x - hi.astype(F32)).astype(BF16)
    return hi, lo


def _head_rms(x, bd):
    x2 = x * x
    hi, lo = _split_bf16(x2)
    cols = []
    for c in range(x.shape[1] // LANES):
        sl = slice(c * LANES, (c + 1) * LANES)
        cols.append(_dot(hi[:, sl], bd) + _dot(lo[:, sl], bd))
    ms = cols[0] if len(cols) == 1 else jnp.concatenate(cols, axis=1)
    return x * lax.rsqrt(ms + EPS)


def _lane_tile(t, width):
    reps = width // t.shape[1]
    return t if reps == 1 else jnp.concatenate([t] * reps, axis=1)


def _rope(x, cos, sin):
    n = x.shape[1]
    fwd = pltpu.roll(x, n - 16, 1)
    bwd = pltpu.roll(x, 16, 1)
    lane = lax.broadcasted_iota(jnp.int32, x.shape, 1)
    swapped = jnp.where((lane & 16) == 0, fwd, bwd)
    return x * _lane_tile(cos, n) + swapped * _lane_tile(sin, n)


def _softmax_parts(s):
    m = jnp.max(s, axis=-1, keepdims=True)
    p = jnp.exp2(s - m)
    return p, jnp.sum(p, axis=-1, keepdims=True)


def _mod_kernel(cc_ref, w_ref, b_ref, o_ref):
    a = _silu(cc_ref[...])
    a_hi, a_lo = _split_bf16(a)
    w_hi, w_lo = _split_bf16(w_ref[0])
    o_ref[0] = _dot(a_hi, w_hi) + _dot(a_hi, w_lo) + _dot(a_lo, w_hi) + b_ref[0]


def _modulation(cc, w_ada, b_ada):
    n_layer, d, d3 = w_ada.shape
    rows = cc.shape[0]
    tn = 768
    return pl.pallas_call(
        _mod_kernel,
        grid=(n_layer, d3 // tn),
        in_specs=[pl.BlockSpec((rows, d), lambda l, n: (0, 0)),
                  pl.BlockSpec((1, d, tn), lambda l, n: (l, 0, n)),
                  pl.BlockSpec((1, 1, tn), lambda l, n: (l, 0, n))],
        out_specs=pl.BlockSpec((1, rows, tn), lambda l, n: (l, 0, n)),
        out_shape=jax.ShapeDtypeStruct((n_layer, rows, d3), F32),
        compiler_params=_cparams("arbitrary", "arbitrary"),
        name="adaln_mod",
    )(cc, w_ada, b_ada.reshape(n_layer, 1, d3))


def _hnorm_kernel(x_ref, mod_ref, g_ref, h_ref):
    d = x_ref.shape[2]
    x = x_ref[0]
    y = x * lax.rsqrt(jnp.mean(x * x, axis=-1, keepdims=True) + EPS) * g_ref[...]
    shift = mod_ref[0, :, 0:d]
    scale = mod_ref[0, :, d:2 * d]
    h_ref[0] = (y * (1.0 + scale) + shift).astype(BF16)


def _hnorm(x, mod, g, per_batch):
    nb, seq, d = x.shape
    tm = min(seq, 1024)
    mod_map = (lambda b, i: (b, 0, 0)) if per_batch else (lambda b, i: (0, 0, 0))
    return pl.pallas_call(
        _hnorm_kernel,
        grid=(nb, seq // tm),
        in_specs=[pl.BlockSpec((1, tm, d), lambda b, i: (b, i, 0)),
                  pl.BlockSpec((1, 1, 3 * d), mod_map),
                  pl.BlockSpec((1, d), lambda b, i: (0, 0))],
        out_specs=pl.BlockSpec((1, tm, d), lambda b, i: (b, i, 0)),
        out_shape=jax.ShapeDtypeStruct((nb, seq, d), BF16),
        compiler_params=_cparams("arbitrary", "arbitrary"),
        name="mod_rmsnorm",
    )(x, mod, g.reshape(1, d))


def _conv_kernel(h_ref, w_ref, cw_ref, cb_ref, lg_ref, lb_ref, o_ref, hcv_ref, *, seq, chunk):
    zeros = jnp.zeros((CONV_PAD, W_A), F32)
    hcv_ref[0:CONV_PAD, :] = zeros
    hcv_ref[seq + CONV_PAD:seq + 2 * CONV_PAD, :] = zeros

    def glu(c, carry):
        r0 = pl.multiple_of(c * chunk, chunk)
        ag = _dot(h_ref[0, pl.ds(r0, chunk), :], w_ref[:, 0:2 * W_A])
        hcv_ref[pl.ds(r0 + CONV_PAD, chunk), :] = ag[:, :W_A] * _sigmoid(ag[:, W_A:])
        return carry

    lax.fori_loop(0, seq // chunk, glu, 0)

    def conv(c, carry):
        r0 = pl.multiple_of(c * chunk, chunk)
        acc = jnp.zeros((chunk, W_A), F32) + cb_ref[...]
        win = hcv_ref[pl.ds(r0, chunk + 2 * CONV_PAD), :]
        n_win = chunk + 2 * CONV_PAD
        for b in range(8):
            offs = [j + CONV_PAD - CONV_K // 2 for j in range(CONV_K) if (j + CONV_PAD - CONV_K // 2) % 8 == b]
            shifted = win if b == 0 else pltpu.roll(win, n_win - b, 0)
            for off in offs:
                j = off - CONV_PAD + CONV_K // 2
                acc = acc + cw_ref[j:j + 1, :] * shifted[off - b:off - b + chunk, :]
        mu = jnp.mean(acc, axis=-1, keepdims=True)
        cen = acc - mu
        var = jnp.mean(cen * cen, axis=-1, keepdims=True)
        y = _silu(cen * lax.rsqrt(var + EPS) * lg_ref[...] + lb_ref[...])
        gate = _dot(h_ref[0, pl.ds(r0, chunk), :], w_ref[:, 2 * W_A:3 * W_A])
        o_ref[0, pl.ds(r0, chunk), :] = (y * _silu(gate)).astype(BF16)
        return carry

    lax.fori_loop(0, seq // chunk, conv, 0)


def _conv_mixer(h, w, conv_w, conv_b, ln_g, ln_b):
    nb, seq, d = h.shape
    chunk = min(seq, 128)
    cw = jnp.concatenate([conv_w, jnp.zeros((32 - CONV_K, W_A), F32)], axis=0)
    return pl.pallas_call(
        functools.partial(_conv_kernel, seq=seq, chunk=chunk),
        grid=(nb,),
        in_specs=[pl.BlockSpec((1, seq, d), lambda b: (b, 0, 0)),
                  _const_spec((d, 3 * W_A)),
                  _const_spec((32, W_A)),
                  _const_spec((1, W_A)), _const_spec((1, W_A)), _const_spec((1, W_A))],
        out_specs=pl.BlockSpec((1, seq, W_A), lambda b: (b, 0, 0)),
        out_shape=jax.ShapeDtypeStruct((nb, seq, W_A), BF16),
        scratch_shapes=[pltpu.VMEM((seq + 2 * CONV_PAD, W_A), F32)],
        compiler_params=_cparams("arbitrary"),
        name="conv_mixer",
    )(h, w, cw, conv_b.reshape(1, W_A), ln_g.reshape(1, W_A), ln_b.reshape(1, W_A))


def _stage_kv(src_ref, n_rows, row0, w_ref, k_off, kw, vw, gk, bd, k_scr, v_scr, k_hd, v_hd, rope_tabs):
    kc = min(n_rows, 512)
    for c in range(n_rows // kc):
        rs = slice(c * kc, (c + 1) * kc)
        kv = _dot(src_ref[0, rs, :], w_ref[:, k_off:k_off + kw + vw])
        k = _head_rms(kv[:, :kw], bd) * gk
        if rope_tabs is not None:
            k = _rope(k, rope_tabs[0][rs, :], rope_tabs[1][rs, :])
        v = kv[:, kw:]
        dst = slice(row0 + c * kc, row0 + (c + 1) * kc)
        for n in range(kw // k_hd):
            k_scr[n, dst, :] = k[:, n * k_hd:(n + 1) * k_hd].astype(BF16)
        for n in range(vw // v_hd):
            v_scr[n, dst, :] = v[:, n * v_hd:(n + 1) * v_hd].astype(BF16)


def _gqa_kernel(*refs, n_lat, n_ctx, n_q, n_kv, tq, rope):
    refs = list(refs)
    h_ref = refs.pop(0) if n_lat else None
    hc_ref = refs.pop(0)
    w_ref, gq_ref, gk_ref, bd_ref = refs[:4]
    refs = refs[4:]
    cos_ref, sin_ref = (refs.pop(0), refs.pop(0)) if rope else (None, None)
    o_ref, k_scr, v_scr = refs
    q_src = h_ref if n_lat else hc_ref
    qw, kw = n_q * HEAD_DIM, n_kv * HEAD_DIM
    group = n_q // n_kv
    bd = bd_ref[...]
    i = pl.program_id(1)

    @pl.when(i == 0)
    def _():
        if n_lat:
            _stage_kv(h_ref, n_lat, 0, w_ref, qw, kw, kw, gk_ref[...], bd, k_scr, v_scr,
                      HEAD_DIM, HEAD_DIM, (cos_ref, sin_ref) if rope else None)
        _stage_kv(hc_ref, n_ctx, n_lat, w_ref, qw, kw, kw, gk_ref[...], bd, k_scr, v_scr,
                  HEAD_DIM, HEAD_DIM, None)

    r0 = pl.multiple_of(i * tq, tq)
    hh = q_src[0, pl.ds(r0, tq), :]
    q = _head_rms(_dot(hh, w_ref[:, 0:qw]), bd) * (gq_ref[...] * (SM_SCALE * LOG2E))
    if rope:
        q = _rope(q, cos_ref[pl.ds(r0, tq), :], sin_ref[pl.ds(r0, tq), :])
    qb = q.astype(BF16)
    gate = _dot(hh, w_ref[:, qw + 2 * kw:qw + 2 * kw + qw])

    outs = []
    for n in range(n_kv):
        heads = [qb[:, (n * group + g) * HEAD_DIM:(n * group + g + 1) * HEAD_DIM] for g in range(group)]
        qn = heads[0] if group == 1 else jnp.concatenate(heads, axis=0)
        p, l = _softmax_parts(_dot_nt(qn, k_scr[n]))
        o = _dot(p.astype(BF16), v_scr[n]) / l
        outs.extend(o[g * tq:(g + 1) * tq] for g in range(group))
    y = jnp.concatenate(outs, axis=1)
    o_ref[0] = (y * _silu(gate)).astype(BF16)


def _gqa(h, hc, w, gq, gk, bd, rope_tabs, n_q, n_kv, tq):
    nb, n_ctx, d = hc.shape
    n_lat = 0 if h is None else h.shape[1]
    sq = n_lat if n_lat else n_ctx
    tq = min(tq, sq)
    qw, kw = n_q * HEAD_DIM, n_kv * HEAD_DIM
    rope = rope_tabs is not None
    full = lambda b, i: (b, 0, 0)
    args, specs = [], []
    if n_lat:
        args.append(h)
        specs.append(pl.BlockSpec((1, n_lat, d), full))
    args.append(hc)
    specs.append(pl.BlockSpec((1, n_ctx, d), full))
    args += [w, jnp.tile(gq, n_q).reshape(1, qw), jnp.tile(gk, n_kv).reshape(1, kw), bd]
    specs += [_const_spec(w.shape), _const_spec((1, qw)), _const_spec((1, kw)), _const_spec(bd.shape)]
    if rope:
        args += list(rope_tabs)
        specs += [_const_spec(rope_tabs[0].shape), _const_spec(rope_tabs[1].shape)]
    return pl.pallas_call(
        functools.partial(_gqa_kernel, n_lat=n_lat, n_ctx=n_ctx, n_q=n_q, n_kv=n_kv, tq=tq, rope=rope),
        grid=(nb, sq // tq),
        in_specs=specs,
        out_specs=pl.BlockSpec((1, tq, qw), lambda b, i: (b, i, 0)),
        out_shape=jax.ShapeDtypeStruct((nb, sq, qw), BF16),
        scratch_shapes=[pltpu.VMEM((n_kv, n_lat + n_ctx, HEAD_DIM), BF16),
                        pltpu.VMEM((n_kv, n_lat + n_ctx, HEAD_DIM), BF16)],
        compiler_params=_cparams("arbitrary", "arbitrary"),
        name="gqa_lat" if n_lat else "gqa_ctx",
    )(*args)


def _diff_kernel(*refs, n_lat, n_ctx, tq, rope, lam_init):
    refs = list(refs)
    h_ref = refs.pop(0) if n_lat else None
    hc_ref = refs.pop(0)
    w_ref, gq_ref, gk_ref, bd_ref, lam_ref, sg_ref = refs[:6]
    refs = refs[6:]
    cos_ref, sin_ref = (refs.pop(0), refs.pop(0)) if rope else (None, None)
    o_ref, k_scr, v_scr = refs
    q_src = h_ref if n_lat else hc_ref
    bd = bd_ref[...]
    i = pl.program_id(1)

    @pl.when(i == 0)
    def _():
        if n_lat:
            _stage_kv(h_ref, n_lat, 0, w_ref, W_D, W_D, W_D, gk_ref[...], bd, k_scr, v_scr,
                      HEAD_DIM, 2 * HEAD_DIM, (cos_ref, sin_ref) if rope else None)
        _stage_kv(hc_ref, n_ctx, n_lat, w_ref, W_D, W_D, W_D, gk_ref[...], bd, k_scr, v_scr,
                  HEAD_DIM, 2 * HEAD_DIM, None)

    lam_p = lam_ref[...]
    lam = (jnp.exp(jnp.sum(lam_p[0:1] * lam_p[1:2], axis=-1, keepdims=True))
           - jnp.exp(jnp.sum(lam_p[2:3] * lam_p[3:4], axis=-1, keepdims=True)) + lam_init)

    r0 = pl.multiple_of(i * tq, tq)
    hh = q_src[0, pl.ds(r0, tq), :]
    q = _head_rms(_dot(hh, w_ref[:, 0:W_D]), bd) * (gq_ref[...] * (SM_SCALE * LOG2E))
    if rope:
        q = _rope(q, cos_ref[pl.ds(r0, tq), :], sin_ref[pl.ds(r0, tq), :])
    qb = q.astype(BF16)
    gate = _dot(hh, w_ref[:, 3 * W_D:4 * W_D])

    outs = []
    for hd in range(H_D):
        es, ls = [], []
        for t in range(2):
            sub = hd * 2 + t
            p, l = _softmax_parts(_dot_nt(qb[:, sub * HEAD_DIM:(sub + 1) * HEAD_DIM], k_scr[sub]))
            es.append(p.astype(BF16))
            ls.append(l)
        pv = _dot(jnp.concatenate(es, axis=0), v_scr[hd])
        o = pv[:tq] / ls[0] - lam * (pv[tq:] / ls[1])
        o = o * lax.rsqrt(jnp.mean(o * o, axis=-1, keepdims=True) + EPS) * sg_ref[...] * (1.0 - lam_init)
        outs.append(o)
    y = jnp.concatenate(outs, axis=1)
    o_ref[0] = (y * _silu(gate)).astype(BF16)


def _diff_attn(h, hc, w, gq, gk, bd, lam_p, subln_g, rope_tabs, lam_init, tq):
    nb, n_ctx, d = hc.shape
    n_lat = 0 if h is None else h.shape[1]
    sq = n_lat if n_lat else n_ctx
    tq = min(tq, sq)
    rope = rope_tabs is not None
    full = lambda b, i: (b, 0, 0)
    args, specs = [], []
    if n_lat:
        args.append(h)
        specs.append(pl.BlockSpec((1, n_lat, d), full))
    args.append(hc)
    specs.append(pl.BlockSpec((1, n_ctx, d), full))
    args += [w, jnp.tile(gq, 2 * H_D).reshape(1, W_D), jnp.tile(gk, 2 * H_D).reshape(1, W_D), bd,
             lam_p, subln_g.reshape(1, 2 * HEAD_DIM)]
    specs += [_const_spec(w.shape), _const_spec((1, W_D)), _const_spec((1, W_D)), _const_spec(bd.shape),
              _const_spec(lam_p.shape), _const_spec((1, 2 * HEAD_DIM))]
    if rope:
        args += list(rope_tabs)
        specs += [_const_spec(rope_tabs[0].shape), _const_spec(rope_tabs[1].shape)]
    return pl.pallas_call(
        functools.partial(_diff_kernel, n_lat=n_lat, n_ctx=n_ctx, tq=tq, rope=rope, lam_init=lam_init),
        grid=(nb, sq // tq),
        in_specs=specs,
        out_specs=pl.BlockSpec((1, tq, W_D), lambda b, i: (b, i, 0)),
        out_shape=jax.ShapeDtypeStruct((nb, sq, W_D), BF16),
        scratch_shapes=[pltpu.VMEM((2 * H_D, n_lat + n_ctx, HEAD_DIM), BF16),
                        pltpu.VMEM((H_D, n_lat + n_ctx, 2 * HEAD_DIM), BF16)],
        compiler_params=_cparams("arbitrary", "arbitrary"),
        name="diff_lat" if n_lat else "diff_ctx",
    )(*args)


def _rpb_table_kernel(rpb_ref, onehot_ref, mask_ref, o_ref):
    r = rpb_ref[...]
    hi = r.astype(BF16)
    mid = (r - hi.astype(F32)).astype(BF16)
    lo = (r - hi.astype(F32) - mid.astype(F32)).astype(BF16)
    oh = onehot_ref[...]
    o_ref[...] = (_dot(hi, oh) + _dot(mid, oh) + _dot(lo, oh) + mask_ref[...]) * LOG2E


def _rpb_tables(na_rpb):
    n_layer = na_rpb.shape[0]
    n_dr, n_dc = 2 * NA_ROWS - 1, 2 * NA_COLS - 1
    cq = np.arange(GRID_W)
    c0 = np.clip(cq - NA_COLS // 2, 0, GRID_W - NA_COLS)
    col_ok = (cq[None, :] >= c0[:, None]) & (cq[None, :] < c0[:, None] + NA_COLS)
    dc = np.clip(cq[None, :] - cq[:, None] + (NA_COLS - 1), 0, n_dc - 1)
    onehot = (np.arange(32)[:, None, None] == dc[None]) & col_ok[None]
    onehot = jnp.asarray(onehot.reshape(32, GRID_W * GRID_W), BF16)
    mask = jnp.asarray(np.where(col_ok, 0.0, NEG_INF).reshape(1, GRID_W * GRID_W), F32)
    rows = n_layer * H_B * n_dr
    rows_pad = -(-rows // 8) * 8
    rpb2 = jnp.zeros((rows_pad, 32), F32).at[:rows, :n_dc].set(na_rpb.reshape(rows, n_dc))
    tab = pl.pallas_call(
        _rpb_table_kernel,
        out_shape=jax.ShapeDtypeStruct((rows_pad, GRID_W * GRID_W), F32),
        compiler_params=pltpu.CompilerParams(vmem_limit_bytes=VMEM_LIMIT),
        name="rpb_table",
    )(rpb2, onehot, mask)
    tab = tab[:rows].reshape(n_layer, H_B, n_dr, GRID_W, GRID_W)
    return jnp.concatenate([tab[:, :, :n_dr - 1], tab[:, :, 1:]], axis=-1)


def _natten_kernel(h_ref, hc_ref, w_ref, gq_ref, gk_ref, bd_ref, tab_ref, o_ref, k_scr, v_scr, *, n_lat, n_ctx):
    bd = bd_ref[...]
    r = pl.program_id(1)
    n_rows = n_lat // GRID_W
    band = NA_ROWS * GRID_W

    @pl.when(r == 0)
    def _():
        _stage_kv(h_ref, n_lat, 0, w_ref, W_B, W_B, W_B, gk_ref[...], bd, k_scr, v_scr, HEAD_DIM, HEAD_DIM, None)
        _stage_kv(hc_ref, n_ctx, n_lat, w_ref, W_B, W_B, W_B, gk_ref[...], bd, k_scr, v_scr, HEAD_DIM, HEAD_DIM, None)

    r_band = jnp.clip(r - NA_ROWS // 2, 0, n_rows - NA_ROWS)
    k0 = pl.multiple_of(r_band * GRID_W, GRID_W)
    j0 = r_band - r + (NA_ROWS - 1)
    q0 = pl.multiple_of(r * GRID_W, GRID_W)
    hh = h_ref[0, pl.ds(q0, GRID_W), :]
    q = _head_rms(_dot(hh, w_ref[:, 0:W_B]), bd) * (gq_ref[...] * (SM_SCALE * LOG2E))
    qb = q.astype(BF16)
    gate = _dot(hh, w_ref[:, 3 * W_B:4 * W_B])

    outs = []
    for hd in range(H_B):
        qh = qb[:, hd * HEAD_DIM:(hd + 1) * HEAD_DIM]
        bias = jnp.concatenate([tab_ref[hd, j0 + 2 * u] for u in range(NA_ROWS // 2)], axis=1)
        s_lat = _dot_nt(qh, k_scr[hd, pl.ds(k0, band), :]) + bias
        s_ctx = _dot_nt(qh, k_scr[hd, n_lat:n_lat + n_ctx, :])
        m = jnp.maximum(jnp.max(s_lat, axis=-1, keepdims=True), jnp.max(s_ctx, axis=-1, keepdims=True))
        p_lat = jnp.exp2(s_lat - m)
        p_ctx = jnp.exp2(s_ctx - m)
        l = jnp.sum(p_lat, axis=-1, keepdims=True) + jnp.sum(p_ctx, axis=-1, keepdims=True)
        o = (_dot(p_lat.astype(BF16), v_scr[hd, pl.ds(k0, band), :])
             + _dot(p_ctx.astype(BF16), v_scr[hd, n_lat:n_lat + n_ctx, :]))
        outs.append(o / l)
    y = jnp.concatenate(outs, axis=1)
    o_ref[0] = (y * _silu(gate)).astype(BF16)


def _natten(h, hc, w, gq, gk, bd, tab):
    nb, n_lat, d = h.shape
    n_ctx = hc.shape[1]
    n_rows = n_lat // GRID_W
    assert n_lat % GRID_W == 0 and n_rows >= NA_ROWS
    full = lambda b, r: (b, 0, 0)
    return pl.pallas_call(
        functools.partial(_natten_kernel, n_lat=n_lat, n_ctx=n_ctx),
        grid=(nb, n_rows),
        in_specs=[pl.BlockSpec((1, n_lat, d), full),
                  pl.BlockSpec((1, n_ctx, d), full),
                  _const_spec(w.shape), _const_spec((1, W_B)), _const_spec((1, W_B)), _const_spec(bd.shape),
                  _const_spec(tab.shape)],
        out_specs=pl.BlockSpec((1, GRID_W, W_B), lambda b, r: (b, r, 0)),
        out_shape=jax.ShapeDtypeStruct((nb, n_lat, W_B), BF16),
        scratch_shapes=[pltpu.VMEM((H_B, n_lat + n_ctx, HEAD_DIM), BF16),
                        pltpu.VMEM((H_B, n_lat + n_ctx, HEAD_DIM), BF16)],
        compiler_params=_cparams("arbitrary", "arbitrary"),
        name="natten",
    )(h, hc, w, jnp.tile(gq, H_B).reshape(1, W_B), jnp.tile(gk, H_B).reshape(1, W_B), bd, tab)


def _merge_kernel(h_ref, ya_ref, yb_ref, yc_ref, yd_ref, wl_ref, bm_ref, wbr_ref, wo_ref, x_ref, mod_ref, o_ref):
    d = x_ref.shape[2]
    hh = h_ref[0]
    m = None
    for br, y_ref in enumerate((ya_ref, yb_ref, yc_ref, yd_ref)):
        cols = slice(br * d, (br + 1) * d)
        g = _sigmoid(_dot(hh, wl_ref[:, cols]) + bm_ref[:, cols])
        t = g * _dot(y_ref[0], wbr_ref[br])
        m = t if m is None else m + t
    gate = mod_ref[0, :, 2 * d:3 * d]
    o_ref[0] = x_ref[0] + gate * _dot(m.astype(BF16), wo_ref[...])


def _merge(h, ys, wl, b_merge, wbr, wo, x, mod, per_batch):
    nb, seq, d = x.shape
    tm = min(seq, 512)
    tile = lambda b, i: (b, i, 0)
    mod_map = (lambda b, i: (b, 0, 0)) if per_batch else (lambda b, i: (0, 0, 0))
    return pl.pallas_call(
        _merge_kernel,
        grid=(nb, seq // tm),
        in_specs=[pl.BlockSpec((1, tm, d), tile)]
                 + [pl.BlockSpec((1, tm, W_A), tile)] * N_BRANCH
                 + [_const_spec(wl.shape), _const_spec((1, N_BRANCH * d)), _const_spec(wbr.shape), _const_spec(wo.shape),
                    pl.BlockSpec((1, tm, d), tile),
                    pl.BlockSpec((1, 1, 3 * d), mod_map)],
        out_specs=pl.BlockSpec((1, tm, d), tile),
        out_shape=jax.ShapeDtypeStruct((nb, seq, d), F32),
        compiler_params=_cparams("arbitrary", "arbitrary"),
        name="merge",
    )(h, *ys, wl, b_merge.reshape(1, N_BRANCH * d), wbr, wo, x, mod)


def _rope_tables(seq):
    half = HEAD_DIM // 2
    t = np.arange(seq)
    freqs = ROPE_THETA ** (-np.arange(0, half, 2, dtype=np.float64) / half)
    blocks_c, blocks_s = [], []
    for pos in (t // GRID_W, t % GRID_W):
        ang = pos[:, None].astype(np.float64) * freqs[None, :]
        blocks_c += [np.cos(ang), np.cos(ang)]
        blocks_s += [-np.sin(ang), np.sin(ang)]
    cos = np.concatenate(blocks_c, axis=1)
    sin = np.concatenate(blocks_s, axis=1)
    reps = LANES // HEAD_DIM
    return jnp.asarray(np.tile(cos, (1, reps)), F32), jnp.asarray(np.tile(sin, (1, reps)), F32)


def _group_mean_matrix():
    idx = np.arange(LANES) // HEAD_DIM
    return jnp.asarray((idx[:, None] == idx[None, :]) / HEAD_DIM, BF16)


def kernel(x, c, ctx, c_ctx, w_ada, b_ada, norm_g, w_in, b_merge, conv_w, conv_b, conv_ln_g, conv_ln_b, na_qn_g, na_kn_g, na_rpb, gqa_qn_g, gqa_kn_g, diff_qn_g, diff_kn_g, lam_q1, lam_k1, lam_q2, lam_k2, diff_subln_g, w_br_a, w_br_b, w_br_c, w_br_d, w_out):
    n_batch, seq, d = x.shape
    depth = w_in.shape[0]
    rope_tabs = _rope_tables(seq)
    bd = _group_mean_matrix()
    tabs = _rpb_tables(na_rpb)

    rows = -(-(n_batch + 1) // 8) * 8
    cc = jnp.zeros((rows, d), F32).at[:n_batch].set(c).at[n_batch].set(c_ctx)
    mod_all = _modulation(cc, w_ada, b_ada)

    xc = ctx
    for l in range(depth):
        need_ctx = l < depth - 1
        lam_init = 0.8 - 0.6 * math.exp(-0.3 * l)
        mod_x = mod_all[l, :n_batch, None, :]
        mod_c = mod_all[l, n_batch:n_batch + 1, None, :]
        h = _hnorm(x, mod_x, norm_g[l], True)
        hc = _hnorm(xc, mod_c, norm_g[l], False)
        w = w_in[l].astype(BF16)
        w_a, w_b, w_c, w_d, w_l = (w[:, OFF_A:OFF_B], w[:, OFF_B:OFF_C], w[:, OFF_C:OFF_D], w[:, OFF_D:OFF_L], w[:, OFF_L:])
        wbr = jnp.stack([w_br_a[l], w_br_b[l], w_br_c[l], w_br_d[l]]).astype(BF16)
        wo = w_out[l].astype(BF16)
        lam_p = jnp.stack([lam_q1[l], lam_k1[l], lam_q2[l], lam_k2[l]])

        ya = _conv_mixer(h, w_a, conv_w[l], conv_b[l], conv_ln_g[l], conv_ln_b[l])
        yb = _natten(h, hc, w_b, na_qn_g[l], na_kn_g[l], bd, tabs[l])
        yc = _gqa(h, hc, w_c, gqa_qn_g[l], gqa_kn_g[l], bd, rope_tabs, H_C, KV_C, 128)
        yd = _diff_attn(h, hc, w_d, diff_qn_g[l], diff_kn_g[l], bd, lam_p, diff_subln_g[l], rope_tabs, lam_init, 256)
        x_new = _merge(h, (ya, yb, yc, yd), w_l, b_merge[l], wbr, wo, x, mod_x, True)

        if need_ctx:
            ca = _conv_mixer(hc, w_a, conv_w[l], conv_b[l], conv_ln_g[l], conv_ln_b[l])
            cb = _gqa(None, hc, w_b, na_qn_g[l], na_kn_g[l], bd, None, H_B, H_B, 256)
            cg = _gqa(None, hc, w_c, gqa_qn_g[l], gqa_kn_g[l], bd, None, H_C, KV_C, 256)
            cd = _diff_attn(None, hc, w_d, diff_qn_g[l], diff_kn_g[l], bd, lam_p, diff_subln_g[l], None, lam_init, 256)
            xc = _merge(hc, (ca, cb, cg, cd), w_l, b_merge[l], wbr, wo, xc, mod_c, False)
        x = x_new
    return x
```

```python
import functools
import math

import numpy as np
import jax
import jax.numpy as jnp
from jax import lax
from jax.experimental import pallas as pl
from jax.experimental.pallas import tpu as pltpu

F32 = jnp.float32
BF16 = jnp.bfloat16

GRID_W = 64
GRID_SHIFT = GRID_W.bit_length() - 1
HEAD_DIM = 64
N_BRANCH = 4
W_A = 512
CONV_K = 31
CONV_PAD = 16
H_B = 8
W_B = H_B * HEAD_DIM
NA_ROWS = 8
NA_COLS = 16
H_C = 8
KV_C = 2
W_C = H_C * HEAD_DIM
W_C_KV = KV_C * HEAD_DIM
H_D = 4
W_D = H_D * 2 * HEAD_DIM
ROPE_THETA = 10000.0
EPS = 1e-6
NEG_INF = -1e30
LOG2E = math.log2(math.e)
SM_SCALE = HEAD_DIM ** -0.5

OFF_A = 0
OFF_B = OFF_A + 3 * W_A
OFF_C = OFF_B + 4 * W_B
OFF_D = OFF_C + 2 * W_C + 2 * W_C_KV
OFF_L = OFF_D + 4 * W_D
LANES = 128
VMEM_LIMIT = 56 * 1024 * 1024


def _cparams(*sem):
    return pltpu.CompilerParams(dimension_semantics=sem, vmem_limit_bytes=VMEM_LIMIT)


def _const_spec(shape):
    nd = len(shape)
    return pl.BlockSpec(shape, lambda *_: (0,) * nd, pipeline_mode=pl.Buffered(1))


def _sigmoid(x):
    return 1.0 / (1.0 + jnp.exp(-x))


def _silu(x):
    return x * _sigmoid(x)


def _dot(a, b):
    return jnp.dot(a, b, preferred_element_type=F32)


def _dot_nt(a, b):
    return lax.dot_general(a, b, (((1,), (1,)), ((), ())), preferred_element_type=F32)


def _split_bf16(x):
    hi = x.astype(BF16)
    lo = (x - hi.astype(F32)).astype(BF16)
    return hi, lo


def _head_rms(x, bd):
    x2 = x * x
    hi, lo = _split_bf16(x2)
    cols = []
    for c in range(x.shape[1] // LANES):
        sl = slice(c * LANES, (c + 1) * LANES)
        cols.append(_dot(hi[:, sl], bd) + _dot(lo[:, sl], bd))
    ms = cols[0] if len(cols) == 1 else jnp.concatenate(cols, axis=1)
    return x * lax.rsqrt(ms + EPS)


def _lane_tile(t, width):
    reps = width // t.shape[1]
    return t if reps == 1 else jnp.concatenate([t] * reps, axis=1)


def _rope(x, cos, sin):
    n = x.shape[1]
    fwd = pltpu.roll(x, n - 16, 1)
    bwd = pltpu.roll(x, 16, 1)
    lane = lax.broadcasted_iota(jnp.int32, x.shape, 1)
    swapped = jnp.where((lane & 16) == 0, fwd, bwd)
    return x * _lane_tile(cos, n) + swapped * _lane_tile(sin, n)


def _mod_kernel(cc_ref, w_ref, b_ref, o_ref):
    a = _silu(cc_ref[...])
    a_hi, a_lo = _split_bf16(a)
    w_hi, w_lo = _split_bf16(w_ref[0])
    o_ref[0] = _dot(a_hi, w_hi) + _dot(a_hi, w_lo) + _dot(a_lo, w_hi) + b_ref[0]


def _modulation(cc, w_ada, b_ada):
    n_layer, d, d3 = w_ada.shape
    rows = cc.shape[0]
    tn = 768
    return pl.pallas_call(
        _mod_kernel,
        grid=(n_layer, d3 // tn),
        in_specs=[pl.BlockSpec((rows, d), lambda l, n: (0, 0)),
                  pl.BlockSpec((1, d, tn), lambda l, n: (l, 0, n)),
                  pl.BlockSpec((1, 1, tn), lambda l, n: (l, 0, n))],
        out_specs=pl.BlockSpec((1, rows, tn), lambda l, n: (l, 0, n)),
        out_shape=jax.ShapeDtypeStruct((n_layer, rows, d3), F32),
        compiler_params=_cparams("arbitrary", "arbitrary"),
        name="adaln_mod",
    )(cc, w_ada, b_ada.reshape(n_layer, 1, d3))


def _hnorm_kernel(x_ref, mod_ref, g_ref, h_ref):
    d = x_ref.shape[2]
    x = x_ref[0]
    y = x * lax.rsqrt(jnp.mean(x * x, axis=-1, keepdims=True) + EPS) * g_ref[...]
    shift = mod_ref[0, :, 0:d]
    scale = mod_ref[0, :, d:2 * d]
    h_ref[0] = (y * (1.0 + scale) + shift).astype(BF16)


def _hnorm(x, mod, g, per_batch):
    nb, seq, d = x.shape
    tm = min(seq, 1024)
    mod_map = (lambda b, i: (b, 0, 0)) if per_batch else (lambda b, i: (0, 0, 0))
    return pl.pallas_call(
        _hnorm_kernel,
        grid=(nb, seq // tm),
        in_specs=[pl.BlockSpec((1, tm, d), lambda b, i: (b, i, 0)),
                  pl.BlockSpec((1, 1, 3 * d), mod_map),
                  pl.BlockSpec((1, d), lambda b, i: (0, 0))],
        out_specs=pl.BlockSpec((1, tm, d), lambda b, i: (b, i, 0)),
        out_shape=jax.ShapeDtypeStruct((nb, seq, d), BF16),
        compiler_params=_cparams("arbitrary", "arbitrary"),
        name="mod_rmsnorm",
    )(x, mod, g.reshape(1, d))


def _conv_kernel(h_ref, w_ref, cw_ref, cb_ref, lg_ref, lb_ref, o_ref, hcv_ref, *, seq, chunk):
    zeros = jnp.zeros((CONV_PAD, W_A), F32)
    hcv_ref[0:CONV_PAD, :] = zeros
    hcv_ref[seq + CONV_PAD:seq + 2 * CONV_PAD, :] = zeros

    def glu(c, carry):
        r0 = pl.multiple_of(c * chunk, chunk)
        ag = _dot(h_ref[0, pl.ds(r0, chunk), :], w_ref[:, 0:2 * W_A])
        hcv_ref[pl.ds(r0 + CONV_PAD, chunk), :] = ag[:, :W_A] * _sigmoid(ag[:, W_A:])
        return carry

    lax.fori_loop(0, seq // chunk, glu, 0)

    def conv(c, carry):
        r0 = pl.multiple_of(c * chunk, chunk)
        acc = jnp.zeros((chunk, W_A), F32) + cb_ref[...]
        win = hcv_ref[pl.ds(r0, chunk + 2 * CONV_PAD), :]
        n_win = chunk + 2 * CONV_PAD
        for b in range(8):
            offs = [j + CONV_PAD - CONV_K // 2 for j in range(CONV_K) if (j + CONV_PAD - CONV_K // 2) % 8 == b]
            shifted = win if b == 0 else pltpu.roll(win, n_win - b, 0)
            for off in offs:
                j = off - CONV_PAD + CONV_K // 2
                acc = acc + cw_ref[j:j + 1, :] * shifted[off - b:off - b + chunk, :]
        mu = jnp.mean(acc, axis=-1, keepdims=True)
        cen = acc - mu
        var = jnp.mean(cen * cen, axis=-1, keepdims=True)
        y = _silu(cen * lax.rsqrt(var + EPS) * lg_ref[...] + lb_ref[...])
        gate = _dot(h_ref[0, pl.ds(r0, chunk), :], w_ref[:, 2 * W_A:3 * W_A])
        o_ref[0, pl.ds(r0, chunk), :] = (y * _silu(gate)).astype(BF16)
        return carry

    lax.fori_loop(0, seq // chunk, conv, 0)


def _conv_mixer(h, w, conv_w, conv_b, ln_g, ln_b):
    nb, seq, d = h.shape
    chunk = min(seq, 128)
    cw = jnp.concatenate([conv_w, jnp.zeros((32 - CONV_K, W_A), F32)], axis=0)
    return pl.pallas_call(
        functools.partial(_conv_kernel, seq=seq, chunk=chunk),
        grid=(nb,),
        in_specs=[pl.BlockSpec((1, seq, d), lambda b: (b, 0, 0)),
                  _const_spec((d, 3 * W_A)),
                  _const_spec((32, W_A)),
                  _const_spec((1, W_A)), _const_spec((1, W_A)), _const_spec((1, W_A))],
        out_specs=pl.BlockSpec((1, seq, W_A), lambda b: (b, 0, 0)),
        out_shape=jax.ShapeDtypeStruct((nb, seq, W_A), BF16),
        scratch_shapes=[pltpu.VMEM((seq + 2 * CONV_PAD, W_A), F32)],
        compiler_params=_cparams("arbitrary"),
        name="conv_mixer",
    )(h, w, cw, conv_b.reshape(1, W_A), ln_g.reshape(1, W_A), ln_b.reshape(1, W_A))


def _stage_k(src_ref, n_rows, row0, wk_ref, gk, bd, k_scr, rope_tabs):
    kc = min(n_rows, 512)
    for c in range(n_rows // kc):
        rs = slice(c * kc, (c + 1) * kc)
        k = _head_rms(_dot(src_ref[0, rs, :], wk_ref[...]), bd) * gk
        if rope_tabs is not None:
            k = _rope(k, rope_tabs[0][rs, :], rope_tabs[1][rs, :])
        dst = slice(row0 + c * kc, row0 + (c + 1) * kc)
        for n in range(k.shape[1] // HEAD_DIM):
            k_scr[n, dst, :] = k[:, n * HEAD_DIM:(n + 1) * HEAD_DIM].astype(BF16)


def _stage_vt(src_ref, n_rows, row0, wvt_ref, vt_scr):
    kc = min(n_rows, 512)
    for c in range(n_rows // kc):
        vt = _dot_nt(wvt_ref[...], src_ref[0, c * kc:(c + 1) * kc, :])
        vt_scr[:, row0 + c * kc:row0 + (c + 1) * kc] = vt.astype(BF16)


def _q_heads_t(hh, wqt_ref, gqt_ref, rope_t):
    qt = _dot_nt(wqt_ref[...], hh)
    tq = qt.shape[1]
    x = qt.reshape(qt.shape[0] // HEAD_DIM, HEAD_DIM, tq)
    x = x * lax.rsqrt(jnp.mean(x * x, axis=1, keepdims=True) + EPS)
    x = x * (_lane_tile(gqt_ref[...], tq) * (SM_SCALE * LOG2E))[None]
    if rope_t is not None:
        swapped = jnp.concatenate([x[:, 16:32], x[:, 0:16], x[:, 48:64], x[:, 32:48]], axis=1)
        x = x * rope_t[0][None] + swapped * rope_t[1][None]
    return x.astype(BF16)


def _softmax_t(st):
    m = jnp.max(st, axis=0, keepdims=True)
    p = jnp.exp2(st - m)
    return p.astype(BF16), jnp.sum(p, axis=0, keepdims=True)


def _attn_args(h, hc, wq, wk, wv, wg, gq, gk, bd):
    nb, n_ctx, d = hc.shape
    n_lat = 0 if h is None else h.shape[1]
    full = lambda b, i: (b, 0, 0)
    args, specs = [], []
    if n_lat:
        args.append(h)
        specs.append(pl.BlockSpec((1, n_lat, d), full))
    args.append(hc)
    specs.append(pl.BlockSpec((1, n_ctx, d), full))
    kw = wk.shape[1]
    gqt = jnp.broadcast_to(gq[:, None], (HEAD_DIM, LANES))
    consts = [wq.T, wk, wv.T, wg, gqt, jnp.tile(gk, kw // HEAD_DIM).reshape(1, kw), bd]
    return args + consts, specs + [_const_spec(a.shape) for a in consts], n_lat, n_ctx


def _rope_args(rope, tq):
    (cos_k, sin_k), (cos_t, sin_t) = rope
    tile = pl.BlockSpec((HEAD_DIM, tq), lambda b, i: (0, i))
    return [cos_k, sin_k, cos_t, sin_t], [_const_spec(cos_k.shape), _const_spec(sin_k.shape), tile, tile]


def _gqa_kernel(*refs, n_lat, n_ctx, n_q, n_kv, tq, rope):
    refs = list(refs)
    h_ref = refs.pop(0) if n_lat else None
    hc_ref = refs.pop(0)
    wqt_ref, wk_ref, wvt_ref, wg_ref, gqt_ref, gk_ref, bd_ref = refs[:7]
    refs = refs[7:]
    cosk_ref = sink_ref = cost_ref = sint_ref = None
    if rope:
        cosk_ref, sink_ref, cost_ref, sint_ref = refs[:4]
        refs = refs[4:]
    o_ref, k_scr, vt_scr = refs
    q_src = h_ref if n_lat else hc_ref
    group = n_q // n_kv
    i = pl.program_id(1)

    @pl.when(i == 0)
    def _():
        if n_lat:
            _stage_k(h_ref, n_lat, 0, wk_ref, gk_ref[...], bd_ref[...], k_scr, (cosk_ref, sink_ref) if rope else None)
            _stage_vt(h_ref, n_lat, 0, wvt_ref, vt_scr)
        _stage_k(hc_ref, n_ctx, n_lat, wk_ref, gk_ref[...], bd_ref[...], k_scr, None)
        _stage_vt(hc_ref, n_ctx, n_lat, wvt_ref, vt_scr)

    r0 = pl.multiple_of(i * tq, tq)
    hh = q_src[0, pl.ds(r0, tq), :]
    qh = _q_heads_t(hh, wqt_ref, gqt_ref, (cost_ref[...], sint_ref[...]) if rope else None)
    gate = _dot(hh, wg_ref[...])

    blocks = [None] * n_q
    for n in range(n_kv):
        heads = [qh[n * group + g] for g in range(group)]
        qn = heads[0] if group == 1 else jnp.concatenate(heads, axis=1)
        p, l = _softmax_t(_dot(k_scr[n], qn))
        ot = _dot(vt_scr[n * HEAD_DIM:(n + 1) * HEAD_DIM, :], p) / l
        for g in range(group):
            blocks[n * group + g] = ot[:, g * tq:(g + 1) * tq]
    yt = jnp.concatenate(blocks, axis=0)
    o_ref[0] = (yt.T * _silu(gate)).astype(BF16)


def _gqa(h, hc, wq, wk, wv, wg, gq, gk, bd, rope, n_q, n_kv, tq):
    args, specs, n_lat, n_ctx = _attn_args(h, hc, wq, wk, wv, wg, gq, gk, bd)
    nb = hc.shape[0]
    sq = n_lat if n_lat else n_ctx
    tq = min(tq, sq)
    qw, kw = n_q * HEAD_DIM, n_kv * HEAD_DIM
    if rope is not None:
        a, s = _rope_args(rope, tq)
        args += a
        specs += s
    return pl.pallas_call(
        functools.partial(_gqa_kernel, n_lat=n_lat, n_ctx=n_ctx, n_q=n_q, n_kv=n_kv, tq=tq, rope=rope is not None),
        grid=(nb, sq // tq),
        in_specs=specs,
        out_specs=pl.BlockSpec((1, tq, qw), lambda b, i: (b, i, 0)),
        out_shape=jax.ShapeDtypeStruct((nb, sq, qw), BF16),
        scratch_shapes=[pltpu.VMEM((n_kv, n_lat + n_ctx, HEAD_DIM), BF16),
                        pltpu.VMEM((kw, n_lat + n_ctx), BF16)],
        compiler_params=_cparams("arbitrary", "arbitrary"),
        name="gqa_lat" if n_lat else "gqa_ctx",
    )(*args)


def _diff_kernel(*refs, n_lat, n_ctx, tq, rope, lam_init):
    refs = list(refs)
    h_ref = refs.pop(0) if n_lat else None
    hc_ref = refs.pop(0)
    wqt_ref, wk_ref, wvt_ref, wg_ref, gqt_ref, gk_ref, bd_ref, lam_ref, sgt_ref = refs[:9]
    refs = refs[9:]
    cosk_ref = sink_ref = cost_ref = sint_ref = None
    if rope:
        cosk_ref, sink_ref, cost_ref, sint_ref = refs[:4]
        refs = refs[4:]
    o_ref, k_scr, vt_scr = refs
    q_src = h_ref if n_lat else hc_ref
    i = pl.program_id(1)

    @pl.when(i == 0)
    def _():
        if n_lat:
            _stage_k(h_ref, n_lat, 0, wk_ref, gk_ref[...], bd_ref[...], k_scr, (cosk_ref, sink_ref) if rope else None)
            _stage_vt(h_ref, n_lat, 0, wvt_ref, vt_scr)
        _stage_k(hc_ref, n_ctx, n_lat, wk_ref, gk_ref[...], bd_ref[...], k_scr, None)
        _stage_vt(hc_ref, n_ctx, n_lat, wvt_ref, vt_scr)

    lam_p = lam_ref[...]
    lam = (jnp.exp(jnp.sum(lam_p[0:1] * lam_p[1:2], axis=-1, keepdims=True))
           - jnp.exp(jnp.sum(lam_p[2:3] * lam_p[3:4], axis=-1, keepdims=True)) + lam_init)

    r0 = pl.multiple_of(i * tq, tq)
    hh = q_src[0, pl.ds(r0, tq), :]
    qh = _q_heads_t(hh, wqt_ref, gqt_ref, (cost_ref[...], sint_ref[...]) if rope else None)
    gate = _dot(hh, wg_ref[...])
    sub_gain = _lane_tile(sgt_ref[...], tq) * (1.0 - lam_init)

    blocks = []
    for hd in range(H_D):
        es, ls = [], []
        for t in range(2):
            p, l = _softmax_t(_dot(k_scr[2 * hd + t], qh[2 * hd + t]))
            es.append(p)
            ls.append(l)
        pv = _dot(vt_scr[hd * 2 * HEAD_DIM:(hd + 1) * 2 * HEAD_DIM, :], jnp.concatenate(es, axis=1))
        o = pv[:, :tq] / ls[0] - lam * (pv[:, tq:] / ls[1])
        blocks.append(o * lax.rsqrt(jnp.mean(o * o, axis=0, keepdims=True) + EPS) * sub_gain)
    yt = jnp.concatenate(blocks, axis=0)
    o_ref[0] = (yt.T * _silu(gate)).astype(BF16)


def _diff_attn(h, hc, wq, wk, wv, wg, gq, gk, bd, lam_p, subln_g, rope, lam_init, tq):
    args, specs, n_lat, n_ctx = _attn_args(h, hc, wq, wk, wv, wg, gq, gk, bd)
    nb = hc.shape[0]
    sq = n_lat if n_lat else n_ctx
    tq = min(tq, sq)
    sgt = jnp.broadcast_to(subln_g[:, None], (2 * HEAD_DIM, LANES))
    args += [lam_p, sgt]
    specs += [_const_spec(lam_p.shape), _const_spec(sgt.shape)]
    if rope is not None:
        a, s = _rope_args(rope, tq)
        args += a
        specs += s
    return pl.pallas_call(
        functools.partial(_diff_kernel, n_lat=n_lat, n_ctx=n_ctx, tq=tq, rope=rope is not None, lam_init=lam_init),
        grid=(nb, sq // tq),
        in_specs=specs,
        out_specs=pl.BlockSpec((1, tq, W_D), lambda b, i: (b, i, 0)),
        out_shape=jax.ShapeDtypeStruct((nb, sq, W_D), BF16),
        scratch_shapes=[pltpu.VMEM((2 * H_D, n_lat + n_ctx, HEAD_DIM), BF16),
                        pltpu.VMEM((W_D, n_lat + n_ctx), BF16)],
        compiler_params=_cparams("arbitrary", "arbitrary"),
        name="diff_lat" if n_lat else "diff_ctx",
    )(*args)


NA_TILE = 4
NA_BAND = NA_TILE + NA_ROWS
NA_VBLK = NA_TILE * GRID_W


def _rpb_table_kernel(rpb_ref, onehot_ref, mask_ref, o_ref):
    r = rpb_ref[...]
    hi = r.astype(BF16)
    mid = (r - hi.astype(F32)).astype(BF16)
    lo = (r - hi.astype(F32) - mid.astype(F32)).astype(BF16)
    oh = onehot_ref[...]
    o_ref[...] = (_dot(hi, oh) + _dot(mid, oh) + _dot(lo, oh) + mask_ref[...]) * LOG2E


def _rpb_tables(na_rpb):
    n_layer = na_rpb.shape[0]
    n_dr, n_dc = 2 * NA_ROWS - 1, 2 * NA_COLS - 1
    cq = np.arange(GRID_W)
    c0 = np.clip(cq - NA_COLS // 2, 0, GRID_W - NA_COLS)
    col_ok = (cq[None, :] >= c0[:, None]) & (cq[None, :] < c0[:, None] + NA_COLS)
    dc = np.clip(cq[None, :] - cq[:, None] + (NA_COLS - 1), 0, n_dc - 1)
    onehot = (np.arange(32)[:, None, None] == dc.T[None]) & col_ok.T[None]
    onehot = jnp.asarray(onehot.reshape(32, GRID_W * GRID_W), BF16)
    mask = jnp.asarray(np.where(col_ok.T, 0.0, NEG_INF).reshape(1, GRID_W * GRID_W), F32)
    rows = n_layer * H_B * n_dr
    rows_pad = -(-rows // 8) * 8
    rpb2 = jnp.zeros((rows_pad, 32), F32).at[:rows, :n_dc].set(na_rpb.reshape(rows, n_dc))
    tab = pl.pallas_call(
        _rpb_table_kernel,
        out_shape=jax.ShapeDtypeStruct((rows_pad, GRID_W * GRID_W), F32),
        compiler_params=pltpu.CompilerParams(vmem_limit_bytes=VMEM_LIMIT),
        name="rpb_table",
    )(rpb2, onehot, mask)
    tab = tab[:rows].reshape(n_layer, H_B, n_dr, GRID_W, GRID_W)
    zero = jnp.zeros_like(tab[:, :, :1])
    padded = jnp.concatenate([zero, tab, zero], axis=2)
    return jnp.concatenate([padded[:, :, 1:], padded[:, :, :-1]], axis=-1)


def _natten_kernel(h_ref, hc_ref, wqt_ref, wk_ref, wvt_ref, wg_ref, gqt_ref, gk_ref, bd_ref, tab_ref,
                   o_ref, k_scr, vt_scr, vtc_scr, *, n_lat, n_ctx):
    t = pl.program_id(1)
    n_rows = n_lat // GRID_W
    tq = NA_TILE * GRID_W
    band = NA_BAND * GRID_W

    @pl.when(t == 0)
    def _():
        _stage_k(h_ref, n_lat, 0, wk_ref, gk_ref[...], bd_ref[...], k_scr, None)
        _stage_k(hc_ref, n_ctx, n_lat, wk_ref, gk_ref[...], bd_ref[...], k_scr, None)
        for c in range(n_lat // NA_VBLK):
            vt_scr[c] = _dot_nt(wvt_ref[...], h_ref[0, c * NA_VBLK:(c + 1) * NA_VBLK, :]).astype(BF16)
        _stage_vt(hc_ref, n_ctx, 0, wvt_ref, vtc_scr)

    blk0 = jnp.clip(t - 1, 0, n_rows // NA_TILE - NA_BAND // NA_TILE)
    u0 = blk0 * NA_TILE
    k0 = pl.multiple_of(u0 * GRID_W, NA_VBLK)

    lane_row = lax.broadcasted_iota(jnp.int32, (1, tq), 1) >> GRID_SHIFT
    mask_rows, bias_idx = [], []
    for u in range(NA_BAND):
        row = jnp.zeros((1, tq), F32)
        for i in range(NA_TILE):
            r_band = jnp.clip(t * NA_TILE + i - NA_ROWS // 2, 0, n_rows - NA_ROWS)
            ok = (u0 + u >= r_band) & (u0 + u < r_band + NA_ROWS)
            row = jnp.where(lane_row == i, jnp.where(ok, 0.0, NEG_INF), row)
        mask_rows.append(row)
        j = u0 + u - t * NA_TILE + (NA_ROWS - 1)
        bias_idx.append([jnp.clip(j - 2 * ip, 0, 2 * NA_ROWS - 1) for ip in range(NA_TILE // 2)])

    q0 = pl.multiple_of(t * tq, tq)
    hh = h_ref[0, pl.ds(q0, tq), :]
    qh = _q_heads_t(hh, wqt_ref, gqt_ref, None)
    gate = _dot(hh, wg_ref[...])

    blocks = []
    for hd in range(H_B):
        st = _dot(k_scr[hd, pl.ds(k0, band), :], qh[hd])
        rows = []
        for u in range(NA_BAND):
            bias = jnp.concatenate([tab_ref[hd, jj] for jj in bias_idx[u]], axis=1)
            rows.append(st[u * GRID_W:(u + 1) * GRID_W, :] + bias + mask_rows[u])
        st = jnp.concatenate(rows, axis=0)
        sc = _dot(k_scr[hd, n_lat:n_lat + n_ctx, :], qh[hd])
        m = jnp.maximum(jnp.max(st, axis=0, keepdims=True), jnp.max(sc, axis=0, keepdims=True))
        p = jnp.exp2(st - m)
        pc = jnp.exp2(sc - m)
        l = jnp.sum(p, axis=0, keepdims=True) + jnp.sum(pc, axis=0, keepdims=True)
        pb = p.astype(BF16)
        hs = slice(hd * HEAD_DIM, (hd + 1) * HEAD_DIM)
        ot = _dot(vtc_scr[hs, :], pc.astype(BF16))
        for c in range(NA_BAND // NA_TILE):
            ot = ot + _dot(vt_scr[blk0 + c, hs, :], pb[c * NA_VBLK:(c + 1) * NA_VBLK, :])
        blocks.append(ot / l)
    yt = jnp.concatenate(blocks, axis=0)
    o_ref[0] = (yt.T * _silu(gate)).astype(BF16)


def _natten(h, hc, wq, wk, wv, wg, gq, gk, bd, tab):
    nb, n_lat, d = h.shape
    n_ctx = hc.shape[1]
    n_rows = n_lat // GRID_W
    assert n_lat % NA_VBLK == 0 and n_rows >= NA_BAND
    tq = NA_TILE * GRID_W
    full = lambda b, t: (b, 0, 0)
    gqt = jnp.broadcast_to(gq[:, None], (HEAD_DIM, LANES))
    consts = [wq.T, wk, wv.T, wg, gqt, jnp.tile(gk, H_B).reshape(1, W_B), bd, tab]
    return pl.pallas_call(
        functools.partial(_natten_kernel, n_lat=n_lat, n_ctx=n_ctx),
        grid=(nb, n_lat // tq),
        in_specs=[pl.BlockSpec((1, n_lat, d), full), pl.BlockSpec((1, n_ctx, d), full)]
                 + [_const_spec(a.shape) for a in consts],
        out_specs=pl.BlockSpec((1, tq, W_B), lambda b, t: (b, t, 0)),
        out_shape=jax.ShapeDtypeStruct((nb, n_lat, W_B), BF16),
        scratch_shapes=[pltpu.VMEM((H_B, n_lat + n_ctx, HEAD_DIM), BF16),
                        pltpu.VMEM((n_lat // NA_VBLK, W_B, NA_VBLK), BF16),
                        pltpu.VMEM((W_B, n_ctx), BF16)],
        compiler_params=_cparams("arbitrary", "arbitrary"),
        name="natten",
    )(h, hc, *consts)


def _merge_kernel(h_ref, ya_ref, yb_ref, yc_ref, yd_ref, wl_ref, bm_ref, wbr_ref, wo_ref, x_ref, mod_ref, o_ref):
    d = x_ref.shape[2]
    hh = h_ref[0]
    m = None
    for br, y_ref in enumerate((ya_ref, yb_ref, yc_ref, yd_ref)):
        cols = slice(br * d, (br + 1) * d)
        g = _sigmoid(_dot(hh, wl_ref[:, cols]) + bm_ref[:, cols])
        t = g * _dot(y_ref[0], wbr_ref[br])
        m = t if m is None else m + t
    gate = mod_ref[0, :, 2 * d:3 * d]
    o_ref[0] = x_ref[0] + gate * _dot(m.astype(BF16), wo_ref[...])


def _merge(h, ys, wl, b_merge, wbr, wo, x, mod, per_batch):
    nb, seq, d = x.shape
    tm = min(seq, 512)
    tile = lambda b, i: (b, i, 0)
    mod_map = (lambda b, i: (b, 0, 0)) if per_batch else (lambda b, i: (0, 0, 0))
    return pl.pallas_call(
        _merge_kernel,
        grid=(nb, seq // tm),
        in_specs=[pl.BlockSpec((1, tm, d), tile)]
                 + [pl.BlockSpec((1, tm, W_A), tile)] * N_BRANCH
                 + [_const_spec(wl.shape), _const_spec((1, N_BRANCH * d)), _const_spec(wbr.shape), _const_spec(wo.shape),
                    pl.BlockSpec((1, tm, d), tile),
                    pl.BlockSpec((1, 1, 3 * d), mod_map)],
        out_specs=pl.BlockSpec((1, tm, d), tile),
        out_shape=jax.ShapeDtypeStruct((nb, seq, d), F32),
        compiler_params=_cparams("arbitrary", "arbitrary"),
        name="merge",
    )(h, *ys, wl, b_merge.reshape(1, N_BRANCH * d), wbr, wo, x, mod)


def _rope_tables(seq):
    half = HEAD_DIM // 2
    t = np.arange(seq)
    freqs = ROPE_THETA ** (-np.arange(0, half, 2, dtype=np.float64) / half)
    blocks_c, blocks_s = [], []
    for pos in (t // GRID_W, t % GRID_W):
        ang = pos[:, None].astype(np.float64) * freqs[None, :]
        blocks_c += [np.cos(ang), np.cos(ang)]
        blocks_s += [-np.sin(ang), np.sin(ang)]
    cos = np.concatenate(blocks_c, axis=1)
    sin = np.concatenate(blocks_s, axis=1)
    reps = LANES // HEAD_DIM
    key_side = (jnp.asarray(np.tile(cos, (1, reps)), F32), jnp.asarray(np.tile(sin, (1, reps)), F32))
    query_side = (jnp.asarray(cos.T, F32), jnp.asarray(sin.T, F32))
    return key_side, query_side


def _group_mean_matrix():
    idx = np.arange(LANES) // HEAD_DIM
    return jnp.asarray((idx[:, None] == idx[None, :]) / HEAD_DIM, BF16)


def _qkvg(w, off, qw, kw, vw, gw):
    edges = np.cumsum([off, qw, kw, vw, gw])
    return tuple(w[:, a:b] for a, b in zip(edges[:-1], edges[1:]))


def kernel(x, c, ctx, c_ctx, w_ada, b_ada, norm_g, w_in, b_merge, conv_w, conv_b, conv_ln_g, conv_ln_b, na_qn_g, na_kn_g, na_rpb, gqa_qn_g, gqa_kn_g, diff_qn_g, diff_kn_g, lam_q1, lam_k1, lam_q2, lam_k2, diff_subln_g, w_br_a, w_br_b, w_br_c, w_br_d, w_out):
    n_batch, seq, d = x.shape
    depth = w_in.shape[0]
    rope = _rope_tables(seq)
    bd = _group_mean_matrix()
    tabs = _rpb_tables(na_rpb)

    rows = -(-(n_batch + 1) // 8) * 8
    cc = jnp.zeros((rows, d), F32).at[:n_batch].set(c).at[n_batch].set(c_ctx)
    mod_all = _modulation(cc, w_ada, b_ada)

    xc = ctx
    for l in range(depth):
        need_ctx = l < depth - 1
        lam_init = 0.8 - 0.6 * math.exp(-0.3 * l)
        mod_x = mod_all[l, :n_batch, None, :]
        mod_c = mod_all[l, n_batch:n_batch + 1, None, :]
        h = _hnorm(x, mod_x, norm_g[l], True)
        hc = _hnorm(xc, mod_c, norm_g[l], False)
        w = w_in[l].astype(BF16)
        w_a = w[:, OFF_A:OFF_B]
        w_b = _qkvg(w, OFF_B, W_B, W_B, W_B, W_B)
        w_c = _qkvg(w, OFF_C, W_C, W_C_KV, W_C_KV, W_C)
        w_d = _qkvg(w, OFF_D, W_D, W_D, W_D, W_D)
        w_l = w[:, OFF_L:]
        wbr = jnp.stack([w_br_a[l], w_br_b[l], w_br_c[l], w_br_d[l]]).astype(BF16)
        wo = w_out[l].astype(BF16)
        lam_p = jnp.stack([lam_q1[l], lam_k1[l], lam_q2[l], lam_k2[l]])

        ya = _conv_mixer(h, w_a, conv_w[l], conv_b[l], conv_ln_g[l], conv_ln_b[l])
        yb = _natten(h, hc, *w_b, na_qn_g[l], na_kn_g[l], bd, tabs[l])
        yc = _gqa(h, hc, *w_c, gqa_qn_g[l], gqa_kn_g[l], bd, rope, H_C, KV_C, 128)
        yd = _diff_attn(h, hc, *w_d, diff_qn_g[l], diff_kn_g[l], bd, lam_p, diff_subln_g[l], rope, lam_init, 512)
        x_new = _merge(h, (ya, yb, yc, yd), w_l, b_merge[l], wbr, wo, x, mod_x, True)

        if need_ctx:
            ca = _conv_mixer(hc, w_a, conv_w[l], conv_b[l], conv_ln_g[l], conv_ln_b[l])
            cb = _gqa(None, hc, *w_b, na_qn_g[l], na_kn_g[l], bd, None, H_B, H_B, 256)
            cg = _gqa(None, hc, *w_c, gqa_qn_g[l], gqa_kn_g[l], bd, None, H_C, KV_C, 256)
            cd = _diff_attn(None, hc, *w_d, diff_qn_g[l], diff_kn_g[l], bd, lam_p, diff_subln_g[l], None, lam_init, 256)
            xc = _merge(hc, (ca, cb, cg, cd), w_l, b_merge[l], wbr, wo, xc, mod_c, False)
        x = x_new
    return x
```

```python
import functools
import math

import numpy as np
import jax
import jax.numpy as jnp
from jax import lax
from jax.experimental import pallas as pl
from jax.experimental.pallas import tpu as pltpu

F32 = jnp.float32
BF16 = jnp.bfloat16

GRID_W = 64
GRID_SHIFT = GRID_W.bit_length() - 1
HEAD_DIM = 64
N_BRANCH = 4
W_A = 512
CONV_K = 31
CONV_PAD = 16
H_B = 8
W_B = H_B * HEAD_DIM
NA_ROWS = 8
NA_COLS = 16
H_C = 8
KV_C = 2
W_C = H_C * HEAD_DIM
W_C_KV = KV_C * HEAD_DIM
H_D = 4
W_D = H_D * 2 * HEAD_DIM
ROPE_THETA = 10000.0
EPS = 1e-6
NEG_INF = -1e30
LOG2E = math.log2(math.e)
SM_SCALE = HEAD_DIM ** -0.5

OFF_A = 0
OFF_B = OFF_A + 3 * W_A
OFF_C = OFF_B + 4 * W_B
OFF_D = OFF_C + 2 * W_C + 2 * W_C_KV
OFF_L = OFF_D + 4 * W_D
LANES = 128
ATTN_LANES = 512
ATTN_KEY_TILE = 256
VMEM_LIMIT = 56 * 1024 * 1024


def _cparams(*sem):
    return pltpu.CompilerParams(dimension_semantics=sem, vmem_limit_bytes=VMEM_LIMIT)


def _const_spec(shape):
    nd = len(shape)
    return pl.BlockSpec(shape, lambda *_: (0,) * nd, pipeline_mode=pl.Buffered(1))


def _sigmoid(x):
    return 1.0 / (1.0 + jnp.exp(-x))


def _silu(x):
    return x * _sigmoid(x)


def _dot(a, b):
    return jnp.dot(a, b, preferred_element_type=F32)


def _dot_nt(a, b):
    return lax.dot_general(a, b, (((1,), (1,)), ((), ())), preferred_element_type=F32)


def _split_bf16(x):
    hi = x.astype(BF16)
    lo = (x - hi.astype(F32)).astype(BF16)
    return hi, lo


def _head_rms(x, bd):
    x2 = x * x
    hi, lo = _split_bf16(x2)
    cols = []
    for c in range(x.shape[1] // LANES):
        sl = slice(c * LANES, (c + 1) * LANES)
        cols.append(_dot(hi[:, sl], bd) + _dot(lo[:, sl], bd))
    ms = cols[0] if len(cols) == 1 else jnp.concatenate(cols, axis=1)
    return x * lax.rsqrt(ms + EPS)


def _lane_tile(t, width):
    reps = width // t.shape[1]
    return t if reps == 1 else jnp.concatenate([t] * reps, axis=1)


def _rope(x, cos, sin):
    n = x.shape[1]
    fwd = pltpu.roll(x, n - 16, 1)
    bwd = pltpu.roll(x, 16, 1)
    lane = lax.broadcasted_iota(jnp.int32, x.shape, 1)
    swapped = jnp.where((lane & 16) == 0, fwd, bwd)
    return x * _lane_tile(cos, n) + swapped * _lane_tile(sin, n)


def _mod_kernel(cc_ref, w_ref, b_ref, o_ref):
    a = _silu(cc_ref[...])
    a_hi, a_lo = _split_bf16(a)
    w_hi, w_lo = _split_bf16(w_ref[0])
    o_ref[0] = _dot(a_hi, w_hi) + _dot(a_hi, w_lo) + _dot(a_lo, w_hi) + b_ref[0]


def _modulation(cc, w_ada, b_ada):
    n_layer, d, d3 = w_ada.shape
    rows = cc.shape[0]
    tn = 768
    return pl.pallas_call(
        _mod_kernel,
        grid=(n_layer, d3 // tn),
        in_specs=[pl.BlockSpec((rows, d), lambda l, n: (0, 0)),
                  pl.BlockSpec((1, d, tn), lambda l, n: (l, 0, n)),
                  pl.BlockSpec((1, 1, tn), lambda l, n: (l, 0, n))],
        out_specs=pl.BlockSpec((1, rows, tn), lambda l, n: (l, 0, n)),
        out_shape=jax.ShapeDtypeStruct((n_layer, rows, d3), F32),
        compiler_params=_cparams("arbitrary", "arbitrary"),
        name="adaln_mod",
    )(cc, w_ada, b_ada.reshape(n_layer, 1, d3))


def _hnorm_kernel(x_ref, mod_ref, g_ref, h_ref):
    d = x_ref.shape[2]
    x = x_ref[0]
    y = x * lax.rsqrt(jnp.mean(x * x, axis=-1, keepdims=True) + EPS) * g_ref[...]
    shift = mod_ref[0, :, 0:d]
    scale = mod_ref[0, :, d:2 * d]
    h_ref[0] = (y * (1.0 + scale) + shift).astype(BF16)


def _hnorm(x, mod, g, per_batch):
    nb, seq, d = x.shape
    tm = min(seq, 1024)
    mod_map = (lambda b, i: (b, 0, 0)) if per_batch else (lambda b, i: (0, 0, 0))
    return pl.pallas_call(
        _hnorm_kernel,
        grid=(nb, seq // tm),
        in_specs=[pl.BlockSpec((1, tm, d), lambda b, i: (b, i, 0)),
                  pl.BlockSpec((1, 1, 3 * d), mod_map),
                  pl.BlockSpec((1, d), lambda b, i: (0, 0))],
        out_specs=pl.BlockSpec((1, tm, d), lambda b, i: (b, i, 0)),
        out_shape=jax.ShapeDtypeStruct((nb, seq, d), BF16),
        compiler_params=_cparams("arbitrary", "arbitrary"),
        name="mod_rmsnorm",
    )(x, mod, g.reshape(1, d))


def _conv_kernel(h_ref, w_ref, cw_ref, cb_ref, lg_ref, lb_ref, o_ref, hcv_ref, *, seq, chunk):
    zeros = jnp.zeros((CONV_PAD, W_A), F32)
    hcv_ref[0:CONV_PAD, :] = zeros
    hcv_ref[seq + CONV_PAD:seq + 2 * CONV_PAD, :] = zeros

    def glu(c, carry):
        r0 = pl.multiple_of(c * chunk, chunk)
        ag = _dot(h_ref[0, pl.ds(r0, chunk), :], w_ref[:, 0:2 * W_A])
        hcv_ref[pl.ds(r0 + CONV_PAD, chunk), :] = ag[:, :W_A] * _sigmoid(ag[:, W_A:])
        return carry

    lax.fori_loop(0, seq // chunk, glu, 0)

    def conv(c, carry):
        r0 = pl.multiple_of(c * chunk, chunk)
        acc = jnp.zeros((chunk, W_A), F32) + cb_ref[...]
        win = hcv_ref[pl.ds(r0, chunk + 2 * CONV_PAD), :]
        n_win = chunk + 2 * CONV_PAD
        for b in range(8):
            offs = [j + CONV_PAD - CONV_K // 2 for j in range(CONV_K) if (j + CONV_PAD - CONV_K // 2) % 8 == b]
            shifted = win if b == 0 else pltpu.roll(win, n_win - b, 0)
            for off in offs:
                j = off - CONV_PAD + CONV_K // 2
                acc = acc + cw_ref[j:j + 1, :] * shifted[off - b:off - b + chunk, :]
        mu = jnp.mean(acc, axis=-1, keepdims=True)
        cen = acc - mu
        var = jnp.mean(cen * cen, axis=-1, keepdims=True)
        y = _silu(cen * lax.rsqrt(var + EPS) * lg_ref[...] + lb_ref[...])
        gate = _dot(h_ref[0, pl.ds(r0, chunk), :], w_ref[:, 2 * W_A:3 * W_A])
        o_ref[0, pl.ds(r0, chunk), :] = (y * _silu(gate)).astype(BF16)
        return carry

    lax.fori_loop(0, seq // chunk, conv, 0)


def _conv_mixer(h, w, conv_w, conv_b, ln_g, ln_b):
    nb, seq, d = h.shape
    chunk = min(seq, 128)
    cw = jnp.concatenate([conv_w, jnp.zeros((32 - CONV_K, W_A), F32)], axis=0)
    return pl.pallas_call(
        functools.partial(_conv_kernel, seq=seq, chunk=chunk),
        grid=(nb,),
        in_specs=[pl.BlockSpec((1, seq, d), lambda b: (b, 0, 0)),
                  _const_spec((d, 3 * W_A)),
                  _const_spec((32, W_A)),
                  _const_spec((1, W_A)), _const_spec((1, W_A)), _const_spec((1, W_A))],
        out_specs=pl.BlockSpec((1, seq, W_A), lambda b: (b, 0, 0)),
        out_shape=jax.ShapeDtypeStruct((nb, seq, W_A), BF16),
        scratch_shapes=[pltpu.VMEM((seq + 2 * CONV_PAD, W_A), F32)],
        compiler_params=_cparams("arbitrary"),
        name="conv_mixer",
    )(h, w, cw, conv_b.reshape(1, W_A), ln_g.reshape(1, W_A), ln_b.reshape(1, W_A))


def _stage_k(src_ref, n_rows, row0, wk_ref, gk, bd, k_scr, rope_tabs):
    kc = min(n_rows, 512)
    for c in range(n_rows // kc):
        rs = slice(c * kc, (c + 1) * kc)
        k = _head_rms(_dot(src_ref[0, rs, :], wk_ref[...]), bd) * gk
        if rope_tabs is not None:
            k = _rope(k, rope_tabs[0][rs, :], rope_tabs[1][rs, :])
        dst = slice(row0 + c * kc, row0 + (c + 1) * kc)
        for n in range(k.shape[1] // HEAD_DIM):
            k_scr[n, dst, :] = k[:, n * HEAD_DIM:(n + 1) * HEAD_DIM].astype(BF16)


def _stage_vt(src_ref, n_rows, row0, wvt_ref, vt_scr):
    kc = min(n_rows, 512)
    for c in range(n_rows // kc):
        vt = _dot_nt(wvt_ref[...], src_ref[0, c * kc:(c + 1) * kc, :])
        vt_scr[:, row0 + c * kc:row0 + (c + 1) * kc] = vt.astype(BF16)


def _q_heads_t(hh, wqt_ref, gqt_ref, rope_t):
    qt = _dot_nt(wqt_ref[...], hh)
    tq = qt.shape[1]
    x = qt.reshape(qt.shape[0] // HEAD_DIM, HEAD_DIM, tq)
    x = x * lax.rsqrt(jnp.mean(x * x, axis=1, keepdims=True) + EPS)
    x = x * (_lane_tile(gqt_ref[...], tq) * (SM_SCALE * LOG2E))[None]
    if rope_t is not None:
        swapped = jnp.concatenate([x[:, 16:32], x[:, 0:16], x[:, 48:64], x[:, 32:48]], axis=1)
        x = x * rope_t[0][None] + swapped * rope_t[1][None]
    return x.astype(BF16)


def _attend(tasks, s_scr, p_scr, n_keys):
    kt = ATTN_KEY_TILE if n_keys % ATTN_KEY_TILE == 0 else LANES
    w = tasks[0][1].shape[1]
    results = []
    m_prev = None
    for step in range(len(tasks) + 1):
        scoring = step < len(tasks)
        reducing = step >= 1
        if scoring:
            k_fn, q_blk, _ = tasks[step]
            m_run = jnp.full((8, w), NEG_INF, F32)
        if reducing:
            v_fn = tasks[step - 1][2]
            l_run = jnp.zeros((8, w), F32)
        for t in range(n_keys // kt):
            rows = slice(t * kt, (t + 1) * kt)
            if scoring:
                s = _dot(k_fn(rows), q_blk)
                s_scr[step % 2, rows, :] = s
                m_run = jnp.maximum(m_run, jnp.max(s.reshape(kt // 8, 8, w), axis=0))
            if reducing:
                p = jnp.exp2(s_scr[(step - 1) % 2, rows, :] - m_prev)
                l_run = l_run + jnp.sum(p.reshape(kt // 8, 8, w), axis=0)
                p_scr[(step - 1) % 2, rows, :] = p.astype(BF16)
        if reducing:
            acc = _dot(v_fn(slice(0, n_keys)), p_scr[(step - 1) % 2])
            results.append((acc, jnp.sum(l_run, axis=0, keepdims=True)))
        if scoring:
            m_prev = jnp.max(m_run, axis=0, keepdims=True)
    return results


def _attn_args(h, hc, wq, wk, wv, wg, gq, gk, bd):
    nb, n_ctx, d = hc.shape
    n_lat = 0 if h is None else h.shape[1]
    full = lambda b, i: (b, 0, 0)
    args, specs = [], []
    if n_lat:
        args.append(h)
        specs.append(pl.BlockSpec((1, n_lat, d), full))
    args.append(hc)
    specs.append(pl.BlockSpec((1, n_ctx, d), full))
    kw = wk.shape[1]
    gqt = jnp.broadcast_to(gq[:, None], (HEAD_DIM, LANES))
    consts = [wq.T, wk, wv.T, wg, gqt, jnp.tile(gk, kw // HEAD_DIM).reshape(1, kw), bd]
    return args + consts, specs + [_const_spec(a.shape) for a in consts], n_lat, n_ctx


def _rope_args(rope, tq):
    (cos_k, sin_k), (cos_t, sin_t) = rope
    tile = pl.BlockSpec((HEAD_DIM, tq), lambda b, i: (0, i))
    return [cos_k, sin_k, cos_t, sin_t], [_const_spec(cos_k.shape), _const_spec(sin_k.shape), tile, tile]


def _gqa_kernel(*refs, n_lat, n_ctx, n_q, n_kv, tq, per_task, rope):
    refs = list(refs)
    h_ref = refs.pop(0) if n_lat else None
    hc_ref = refs.pop(0)
    wqt_ref, wk_ref, wvt_ref, wg_ref, gqt_ref, gk_ref, bd_ref = refs[:7]
    refs = refs[7:]
    cosk_ref = sink_ref = cost_ref = sint_ref = None
    if rope:
        cosk_ref, sink_ref, cost_ref, sint_ref = refs[:4]
        refs = refs[4:]
    o_ref, k_scr, vt_scr, s_scr, p_scr = refs
    q_src = h_ref if n_lat else hc_ref
    group = n_q // n_kv
    i = pl.program_id(1)

    @pl.when(i == 0)
    def _():
        if n_lat:
            _stage_k(h_ref, n_lat, 0, wk_ref, gk_ref[...], bd_ref[...], k_scr, (cosk_ref, sink_ref) if rope else None)
            _stage_vt(h_ref, n_lat, 0, wvt_ref, vt_scr)
        _stage_k(hc_ref, n_ctx, n_lat, wk_ref, gk_ref[...], bd_ref[...], k_scr, None)
        _stage_vt(hc_ref, n_ctx, n_lat, wvt_ref, vt_scr)

    r0 = pl.multiple_of(i * tq, tq)
    hh = q_src[0, pl.ds(r0, tq), :]
    qh = _q_heads_t(hh, wqt_ref, gqt_ref, (cost_ref[...], sint_ref[...]) if rope else None)
    gate = _dot(hh, wg_ref[...])

    tasks = []
    for n in range(n_kv):
        for first in range(0, group, per_task):
            heads = [qh[n * group + first + g] for g in range(per_task)]
            q_blk = heads[0] if per_task == 1 else jnp.concatenate(heads, axis=1)
            tasks.append((lambda rows, n=n: k_scr[n, rows, :], q_blk,
                          lambda rows, n=n: vt_scr[n * HEAD_DIM:(n + 1) * HEAD_DIM, rows]))
    blocks = []
    for acc, l in _attend(tasks, s_scr, p_scr, n_lat + n_ctx):
        ot = acc / l
        blocks.extend(ot[:, g * tq:(g + 1) * tq] for g in range(per_task))
    yt = jnp.concatenate(blocks, axis=0)
    o_ref[0] = (yt.T * _silu(gate)).astype(BF16)


def _gqa(h, hc, wq, wk, wv, wg, gq, gk, bd, rope, n_q, n_kv, tq):
    args, specs, n_lat, n_ctx = _attn_args(h, hc, wq, wk, wv, wg, gq, gk, bd)
    nb = hc.shape[0]
    sq = n_lat if n_lat else n_ctx
    tq = min(tq, sq)
    qw, kw = n_q * HEAD_DIM, n_kv * HEAD_DIM
    per_task = max(1, min(n_q // n_kv, ATTN_LANES // tq))
    if rope is not None:
        a, s = _rope_args(rope, tq)
        args += a
        specs += s
    return pl.pallas_call(
        functools.partial(_gqa_kernel, n_lat=n_lat, n_ctx=n_ctx, n_q=n_q, n_kv=n_kv, tq=tq, per_task=per_task,
                          rope=rope is not None),
        grid=(nb, sq // tq),
        in_specs=specs,
        out_specs=pl.BlockSpec((1, tq, qw), lambda b, i: (b, i, 0)),
        out_shape=jax.ShapeDtypeStruct((nb, sq, qw), BF16),
        scratch_shapes=[pltpu.VMEM((n_kv, n_lat + n_ctx, HEAD_DIM), BF16),
                        pltpu.VMEM((kw, n_lat + n_ctx), BF16),
                        pltpu.VMEM((2, n_lat + n_ctx, per_task * tq), F32),
                        pltpu.VMEM((2, n_lat + n_ctx, per_task * tq), BF16)],
        compiler_params=_cparams("arbitrary", "arbitrary"),
        name="gqa_lat" if n_lat else "gqa_ctx",
    )(*args)


def _diff_kernel(*refs, n_lat, n_ctx, tq, rope, lam_init):
    refs = list(refs)
    h_ref = refs.pop(0) if n_lat else None
    hc_ref = refs.pop(0)
    wqt_ref, wk_ref, wvt_ref, wg_ref, gqt_ref, gk_ref, bd_ref, lam_ref, sgt_ref = refs[:9]
    refs = refs[9:]
    cosk_ref = sink_ref = cost_ref = sint_ref = None
    if rope:
        cosk_ref, sink_ref, cost_ref, sint_ref = refs[:4]
        refs = refs[4:]
    o_ref, k_scr, vt_scr, s_scr, p_scr = refs
    q_src = h_ref if n_lat else hc_ref
    i = pl.program_id(1)

    @pl.when(i == 0)
    def _():
        if n_lat:
            _stage_k(h_ref, n_lat, 0, wk_ref, gk_ref[...], bd_ref[...], k_scr, (cosk_ref, sink_ref) if rope else None)
            _stage_vt(h_ref, n_lat, 0, wvt_ref, vt_scr)
        _stage_k(hc_ref, n_ctx, n_lat, wk_ref, gk_ref[...], bd_ref[...], k_scr, None)
        _stage_vt(hc_ref, n_ctx, n_lat, wvt_ref, vt_scr)

    lam_p = lam_ref[...]
    lam = (jnp.exp(jnp.sum(lam_p[0:1] * lam_p[1:2], axis=-1, keepdims=True))
           - jnp.exp(jnp.sum(lam_p[2:3] * lam_p[3:4], axis=-1, keepdims=True)) + lam_init)

    r0 = pl.multiple_of(i * tq, tq)
    hh = q_src[0, pl.ds(r0, tq), :]
    qh = _q_heads_t(hh, wqt_ref, gqt_ref, (cost_ref[...], sint_ref[...]) if rope else None)
    gate = _dot(hh, wg_ref[...])
    sub_gain = _lane_tile(sgt_ref[...], tq) * (1.0 - lam_init)

    tasks = []
    for sub in range(2 * H_D):
        v_rows = slice((sub // 2) * 2 * HEAD_DIM, (sub // 2 + 1) * 2 * HEAD_DIM)
        tasks.append((lambda rows, sub=sub: k_scr[sub, rows, :], qh[sub],
                      lambda rows, v_rows=v_rows: vt_scr[v_rows, rows]))
    res = _attend(tasks, s_scr, p_scr, n_lat + n_ctx)
    blocks = []
    for hd in range(H_D):
        (pv0, l0), (pv1, l1) = res[2 * hd], res[2 * hd + 1]
        o = pv0 / l0 - lam * (pv1 / l1)
        blocks.append(o * lax.rsqrt(jnp.mean(o * o, axis=0, keepdims=True) + EPS) * sub_gain)
    yt = jnp.concatenate(blocks, axis=0)
    o_ref[0] = (yt.T * _silu(gate)).astype(BF16)


def _diff_attn(h, hc, wq, wk, wv, wg, gq, gk, bd, lam_p, subln_g, rope, lam_init, tq):
    args, specs, n_lat, n_ctx = _attn_args(h, hc, wq, wk, wv, wg, gq, gk, bd)
    nb = hc.shape[0]
    sq = n_lat if n_lat else n_ctx
    tq = min(tq, sq)
    sgt = jnp.broadcast_to(subln_g[:, None], (2 * HEAD_DIM, LANES))
    args += [lam_p, sgt]
    specs += [_const_spec(lam_p.shape), _const_spec(sgt.shape)]
    if rope is not None:
        a, s = _rope_args(rope, tq)
        args += a
        specs += s
    return pl.pallas_call(
        functools.partial(_diff_kernel, n_lat=n_lat, n_ctx=n_ctx, tq=tq, rope=rope is not None, lam_init=lam_init),
        grid=(nb, sq // tq),
        in_specs=specs,
        out_specs=pl.BlockSpec((1, tq, W_D), lambda b, i: (b, i, 0)),
        out_shape=jax.ShapeDtypeStruct((nb, sq, W_D), BF16),
        scratch_shapes=[pltpu.VMEM((2 * H_D, n_lat + n_ctx, HEAD_DIM), BF16),
                        pltpu.VMEM((W_D, n_lat + n_ctx), BF16),
                        pltpu.VMEM((2, n_lat + n_ctx, tq), F32),
                        pltpu.VMEM((2, n_lat + n_ctx, tq), BF16)],
        compiler_params=_cparams("arbitrary", "arbitrary"),
        name="diff_lat" if n_lat else "diff_ctx",
    )(*args)


NA_TILE = 4
NA_BAND = NA_TILE + NA_ROWS
NA_VBLK = NA_TILE * GRID_W


def _rpb_table_kernel(rpb_ref, onehot_ref, mask_ref, o_ref):
    r = rpb_ref[...]
    hi = r.astype(BF16)
    mid = (r - hi.astype(F32)).astype(BF16)
    lo = (r - hi.astype(F32) - mid.astype(F32)).astype(BF16)
    oh = onehot_ref[...]
    o_ref[...] = (_dot(hi, oh) + _dot(mid, oh) + _dot(lo, oh) + mask_ref[...]) * LOG2E


def _rpb_tables(na_rpb):
    n_layer = na_rpb.shape[0]
    n_dr, n_dc = 2 * NA_ROWS - 1, 2 * NA_COLS - 1
    cq = np.arange(GRID_W)
    c0 = np.clip(cq - NA_COLS // 2, 0, GRID_W - NA_COLS)
    col_ok = (cq[None, :] >= c0[:, None]) & (cq[None, :] < c0[:, None] + NA_COLS)
    dc = np.clip(cq[None, :] - cq[:, None] + (NA_COLS - 1), 0, n_dc - 1)
    onehot = (np.arange(32)[:, None, None] == dc.T[None]) & col_ok.T[None]
    onehot = jnp.asarray(onehot.reshape(32, GRID_W * GRID_W), BF16)
    mask = jnp.asarray(np.where(col_ok.T, 0.0, NEG_INF).reshape(1, GRID_W * GRID_W), F32)
    rows = n_layer * H_B * n_dr
    rows_pad = -(-rows // 8) * 8
    rpb2 = jnp.zeros((rows_pad, 32), F32).at[:rows, :n_dc].set(na_rpb.reshape(rows, n_dc))
    tab = pl.pallas_call(
        _rpb_table_kernel,
        out_shape=jax.ShapeDtypeStruct((rows_pad, GRID_W * GRID_W), F32),
        compiler_params=pltpu.CompilerParams(vmem_limit_bytes=VMEM_LIMIT),
        name="rpb_table",
    )(rpb2, onehot, mask)
    tab = tab[:rows].reshape(n_layer, H_B, n_dr, GRID_W, GRID_W)
    zero = jnp.zeros_like(tab[:, :, :1])
    padded = jnp.concatenate([zero, tab, zero], axis=2)
    return jnp.concatenate([padded[:, :, 1:], padded[:, :, :-1]], axis=-1)


def _natten_kernel(h_ref, hc_ref, wqt_ref, wk_ref, wvt_ref, wg_ref, gqt_ref, gk_ref, bd_ref, tab_ref,
                   o_ref, k_scr, vt_scr, vtc_scr, *, n_lat, n_ctx):
    t = pl.program_id(1)
    n_rows = n_lat // GRID_W
    tq = NA_TILE * GRID_W
    band = NA_BAND * GRID_W

    @pl.when(t == 0)
    def _():
        _stage_k(h_ref, n_lat, 0, wk_ref, gk_ref[...], bd_ref[...], k_scr, None)
        _stage_k(hc_ref, n_ctx, n_lat, wk_ref, gk_ref[...], bd_ref[...], k_scr, None)
        for c in range(n_lat // NA_VBLK):
            vt_scr[c] = _dot_nt(wvt_ref[...], h_ref[0, c * NA_VBLK:(c + 1) * NA_VBLK, :]).astype(BF16)
        _stage_vt(hc_ref, n_ctx, 0, wvt_ref, vtc_scr)

    blk0 = jnp.clip(t - 1, 0, n_rows // NA_TILE - NA_BAND // NA_TILE)
    u0 = blk0 * NA_TILE
    k0 = pl.multiple_of(u0 * GRID_W, NA_VBLK)

    lane_row = lax.broadcasted_iota(jnp.int32, (1, tq), 1) >> GRID_SHIFT
    mask_rows, bias_idx = [], []
    for u in range(NA_BAND):
        row = jnp.zeros((1, tq), F32)
        for i in range(NA_TILE):
            r_band = jnp.clip(t * NA_TILE + i - NA_ROWS // 2, 0, n_rows - NA_ROWS)
            ok = (u0 + u >= r_band) & (u0 + u < r_band + NA_ROWS)
            row = jnp.where(lane_row == i, jnp.where(ok, 0.0, NEG_INF), row)
        mask_rows.append(row)
        j = u0 + u - t * NA_TILE + (NA_ROWS - 1)
        bias_idx.append([jnp.clip(j - 2 * ip, 0, 2 * NA_ROWS - 1) for ip in range(NA_TILE // 2)])

    q0 = pl.multiple_of(t * tq, tq)
    hh = h_ref[0, pl.ds(q0, tq), :]
    qh = _q_heads_t(hh, wqt_ref, gqt_ref, None)
    gate = _dot(hh, wg_ref[...])

    def scores(hd):
        return (_dot(k_scr[hd, pl.ds(k0, band), :], qh[hd]),
                _dot(k_scr[hd, n_lat:n_lat + n_ctx, :], qh[hd]))

    blocks = []
    ahead = scores(0)
    for hd in range(H_B):
        st, sc = ahead
        if hd + 1 < H_B:
            ahead = scores(hd + 1)
        rows = []
        for u in range(NA_BAND):
            bias = jnp.concatenate([tab_ref[hd, jj] for jj in bias_idx[u]], axis=1)
            rows.append(st[u * GRID_W:(u + 1) * GRID_W, :] + bias + mask_rows[u])
        st = jnp.concatenate(rows, axis=0)
        m = jnp.maximum(jnp.max(st, axis=0, keepdims=True), jnp.max(sc, axis=0, keepdims=True))
        p = jnp.exp2(st - m)
        pc = jnp.exp2(sc - m)
        l = jnp.sum(p, axis=0, keepdims=True) + jnp.sum(pc, axis=0, keepdims=True)
        pb = p.astype(BF16)
        hs = slice(hd * HEAD_DIM, (hd + 1) * HEAD_DIM)
        ot = _dot(vtc_scr[hs, :], pc.astype(BF16))
        for c in range(NA_BAND // NA_TILE):
            ot = ot + _dot(vt_scr[blk0 + c, hs, :], pb[c * NA_VBLK:(c + 1) * NA_VBLK, :])
        blocks.append(ot / l)
    yt = jnp.concatenate(blocks, axis=0)
    o_ref[0] = (yt.T * _silu(gate)).astype(BF16)


def _natten(h, hc, wq, wk, wv, wg, gq, gk, bd, tab):
    nb, n_lat, d = h.shape
    n_ctx = hc.shape[1]
    n_rows = n_lat // GRID_W
    assert n_lat % NA_VBLK == 0 and n_rows >= NA_BAND
    tq = NA_TILE * GRID_W
    full = lambda b, t: (b, 0, 0)
    gqt = jnp.broadcast_to(gq[:, None], (HEAD_DIM, LANES))
    consts = [wq.T, wk, wv.T, wg, gqt, jnp.tile(gk, H_B).reshape(1, W_B), bd, tab]
    return pl.pallas_call(
        functools.partial(_natten_kernel, n_lat=n_lat, n_ctx=n_ctx),
        grid=(nb, n_lat // tq),
        in_specs=[pl.BlockSpec((1, n_lat, d), full), pl.BlockSpec((1, n_ctx, d), full)]
                 + [_const_spec(a.shape) for a in consts],
        out_specs=pl.BlockSpec((1, tq, W_B), lambda b, t: (b, t, 0)),
        out_shape=jax.ShapeDtypeStruct((nb, n_lat, W_B), BF16),
        scratch_shapes=[pltpu.VMEM((H_B, n_lat + n_ctx, HEAD_DIM), BF16),
                        pltpu.VMEM((n_lat // NA_VBLK, W_B, NA_VBLK), BF16),
                        pltpu.VMEM((W_B, n_ctx), BF16)],
        compiler_params=_cparams("arbitrary", "arbitrary"),
        name="natten",
    )(h, hc, *consts)


def _merge_kernel(h_ref, ya_ref, yb_ref, yc_ref, yd_ref, wl_ref, bm_ref, wbr_ref, wo_ref, x_ref, mod_ref, o_ref):
    d = x_ref.shape[2]
    hh = h_ref[0]
    m = None
    for br, y_ref in enumerate((ya_ref, yb_ref, yc_ref, yd_ref)):
        cols = slice(br * d, (br + 1) * d)
        g = _sigmoid(_dot(hh, wl_ref[:, cols]) + bm_ref[:, cols])
        t = g * _dot(y_ref[0], wbr_ref[br])
        m = t if m is None else m + t
    gate = mod_ref[0, :, 2 * d:3 * d]
    o_ref[0] = x_ref[0] + gate * _dot(m.astype(BF16), wo_ref[...])


def _merge(h, ys, wl, b_merge, wbr, wo, x, mod, per_batch):
    nb, seq, d = x.shape
    tm = min(seq, 512)
    tile = lambda b, i: (b, i, 0)
    mod_map = (lambda b, i: (b, 0, 0)) if per_batch else (lambda b, i: (0, 0, 0))
    return pl.pallas_call(
        _merge_kernel,
        grid=(nb, seq // tm),
        in_specs=[pl.BlockSpec((1, tm, d), tile)]
                 + [pl.BlockSpec((1, tm, W_A), tile)] * N_BRANCH
                 + [_const_spec(wl.shape), _const_spec((1, N_BRANCH * d)), _const_spec(wbr.shape), _const_spec(wo.shape),
                    pl.BlockSpec((1, tm, d), tile),
                    pl.BlockSpec((1, 1, 3 * d), mod_map)],
        out_specs=pl.BlockSpec((1, tm, d), tile),
        out_shape=jax.ShapeDtypeStruct((nb, seq, d), F32),
        compiler_params=_cparams("arbitrary", "arbitrary"),
        name="merge",
    )(h, *ys, wl, b_merge.reshape(1, N_BRANCH * d), wbr, wo, x, mod)


def _rope_tables(seq):
    half = HEAD_DIM // 2
    t = np.arange(seq)
    freqs = ROPE_THETA ** (-np.arange(0, half, 2, dtype=np.float64) / half)
    blocks_c, blocks_s = [], []
    for pos in (t // GRID_W, t % GRID_W):
        ang = pos[:, None].astype(np.float64) * freqs[None, :]
        blocks_c += [np.cos(ang), np.cos(ang)]
        blocks_s += [-np.sin(ang), np.sin(ang)]
    cos = np.concatenate(blocks_c, axis=1)
    sin = np.concatenate(blocks_s, axis=1)
    reps = LANES // HEAD_DIM
    key_side = (jnp.asarray(np.tile(cos, (1, reps)), F32), jnp.asarray(np.tile(sin, (1, reps)), F32))
    query_side = (jnp.asarray(cos.T, F32), jnp.asarray(sin.T, F32))
    return key_side, query_side


def _group_mean_matrix():
    idx = np.arange(LANES) // HEAD_DIM
    return jnp.asarray((idx[:, None] == idx[None, :]) / HEAD_DIM, BF16)


def _qkvg(w, off, qw, kw, vw, gw):
    edges = np.cumsum([off, qw, kw, vw, gw])
    return tuple(w[:, a:b] for a, b in zip(edges[:-1], edges[1:]))


def kernel(x, c, ctx, c_ctx, w_ada, b_ada, norm_g, w_in, b_merge, conv_w, conv_b, conv_ln_g, conv_ln_b, na_qn_g, na_kn_g, na_rpb, gqa_qn_g, gqa_kn_g, diff_qn_g, diff_kn_g, lam_q1, lam_k1, lam_q2, lam_k2, diff_subln_g, w_br_a, w_br_b, w_br_c, w_br_d, w_out):
    n_batch, seq, d = x.shape
    depth = w_in.shape[0]
    rope = _rope_tables(seq)
    bd = _group_mean_matrix()
    tabs = _rpb_tables(na_rpb)

    rows = -(-(n_batch + 1) // 8) * 8
    cc = jnp.zeros((rows, d), F32).at[:n_batch].set(c).at[n_batch].set(c_ctx)
    mod_all = _modulation(cc, w_ada, b_ada)

    xc = ctx
    for l in range(depth):
        need_ctx = l < depth - 1
        lam_init = 0.8 - 0.6 * math.exp(-0.3 * l)
        mod_x = mod_all[l, :n_batch, None, :]
        mod_c = mod_all[l, n_batch:n_batch + 1, None, :]
        h = _hnorm(x, mod_x, norm_g[l], True)
        hc = _hnorm(xc, mod_c, norm_g[l], False)
        w = w_in[l].astype(BF16)
        w_a = w[:, OFF_A:OFF_B]
        w_b = _qkvg(w, OFF_B, W_B, W_B, W_B, W_B)
        w_c = _qkvg(w, OFF_C, W_C, W_C_KV, W_C_KV, W_C)
        w_d = _qkvg(w, OFF_D, W_D, W_D, W_D, W_D)
        w_l = w[:, OFF_L:]
        wbr = jnp.stack([w_br_a[l], w_br_b[l], w_br_c[l], w_br_d[l]]).astype(BF16)
        wo = w_out[l].astype(BF16)
        lam_p = jnp.stack([lam_q1[l], lam_k1[l], lam_q2[l], lam_k2[l]])

        ya = _conv_mixer(h, w_a, conv_w[l], conv_b[l], conv_ln_g[l], conv_ln_b[l])
        yb = _natten(h, hc, *w_b, na_qn_g[l], na_kn_g[l], bd, tabs[l])
        yc = _gqa(h, hc, *w_c, gqa_qn_g[l], gqa_kn_g[l], bd, rope, H_C, KV_C, 256)
        yd = _diff_attn(h, hc, *w_d, diff_qn_g[l], diff_kn_g[l], bd, lam_p, diff_subln_g[l], rope, lam_init, 512)
        x_new = _merge(h, (ya, yb, yc, yd), w_l, b_merge[l], wbr, wo, x, mod_x, True)

        if need_ctx:
            ca = _conv_mixer(hc, w_a, conv_w[l], conv_b[l], conv_ln_g[l], conv_ln_b[l])
            cb = _gqa(None, hc, *w_b, na_qn_g[l], na_kn_g[l], bd, None, H_B, H_B, 256)
            cg = _gqa(None, hc, *w_c, gqa_qn_g[l], gqa_kn_g[l], bd, None, H_C, KV_C, 256)
            cd = _diff_attn(None, hc, *w_d, diff_qn_g[l], diff_kn_g[l], bd, lam_p, diff_subln_g[l], None, lam_init, 256)
            xc = _merge(hc, (ca, cb, cg, cd), w_l, b_merge[l], wbr, wo, xc, mod_c, False)
        x = x_new
    return x
```

```python
import functools
import math

import numpy as np
import jax
import jax.numpy as jnp
from jax import lax
from jax.experimental import pallas as pl
from jax.experimental.pallas import tpu as pltpu

F32 = jnp.float32
BF16 = jnp.bfloat16

GRID_W = 64
GRID_SHIFT = GRID_W.bit_length() - 1
HEAD_DIM = 64
N_BRANCH = 4
W_A = 512
CONV_K = 31
CONV_PAD = 16
H_B = 8
W_B = H_B * HEAD_DIM
NA_ROWS = 8
NA_COLS = 16
H_C = 8
KV_C = 2
W_C = H_C * HEAD_DIM
W_C_KV = KV_C * HEAD_DIM
H_D = 4
W_D = H_D * 2 * HEAD_DIM
ROPE_THETA = 10000.0
EPS = 1e-6
NEG_INF = -1e30
LOG2E = math.log2(math.e)
SM_SCALE = HEAD_DIM ** -0.5

OFF_A = 0
OFF_B = OFF_A + 3 * W_A
OFF_C = OFF_B + 4 * W_B
OFF_D = OFF_C + 2 * W_C + 2 * W_C_KV
OFF_L = OFF_D + 4 * W_D
LANES = 128
ATTN_LANES = 512
ATTN_KEY_TILE = 256
VMEM_LIMIT = 56 * 1024 * 1024


def _cparams(*sem):
    return pltpu.CompilerParams(dimension_semantics=sem, vmem_limit_bytes=VMEM_LIMIT)


def _const_spec(shape):
    nd = len(shape)
    return pl.BlockSpec(shape, lambda *_: (0,) * nd, pipeline_mode=pl.Buffered(1))


def _sigmoid(x):
    return 1.0 / (1.0 + jnp.exp(-x))


def _silu(x):
    return x * _sigmoid(x)


def _dot(a, b):
    return jnp.dot(a, b, preferred_element_type=F32)


def _dot_nt(a, b):
    return lax.dot_general(a, b, (((1,), (1,)), ((), ())), preferred_element_type=F32)


def _split_bf16(x):
    hi = x.astype(BF16)
    lo = (x - hi.astype(F32)).astype(BF16)
    return hi, lo


def _head_rms(x, bd):
    x2 = x * x
    hi, lo = _split_bf16(x2)
    cols = []
    for c in range(x.shape[1] // LANES):
        sl = slice(c * LANES, (c + 1) * LANES)
        cols.append(_dot(hi[:, sl], bd) + _dot(lo[:, sl], bd))
    ms = cols[0] if len(cols) == 1 else jnp.concatenate(cols, axis=1)
    return x * lax.rsqrt(ms + EPS)


def _lane_tile(t, width):
    reps = width // t.shape[1]
    return t if reps == 1 else jnp.concatenate([t] * reps, axis=1)


def _rope(x, cos, sin):
    n = x.shape[1]
    fwd = pltpu.roll(x, n - 16, 1)
    bwd = pltpu.roll(x, 16, 1)
    lane = lax.broadcasted_iota(jnp.int32, x.shape, 1)
    swapped = jnp.where((lane & 16) == 0, fwd, bwd)
    return x * _lane_tile(cos, n) + swapped * _lane_tile(sin, n)


def _mod_kernel(cc_ref, w_ref, b_ref, o_ref):
    a = _silu(cc_ref[...])
    a_hi, a_lo = _split_bf16(a)
    w_hi, w_lo = _split_bf16(w_ref[0])
    o_ref[0] = _dot(a_hi, w_hi) + _dot(a_hi, w_lo) + _dot(a_lo, w_hi) + b_ref[0]


def _modulation(cc, w_ada, b_ada):
    n_layer, d, d3 = w_ada.shape
    rows = cc.shape[0]
    tn = 768
    return pl.pallas_call(
        _mod_kernel,
        grid=(n_layer, d3 // tn),
        in_specs=[pl.BlockSpec((rows, d), lambda l, n: (0, 0)),
                  pl.BlockSpec((1, d, tn), lambda l, n: (l, 0, n)),
                  pl.BlockSpec((1, 1, tn), lambda l, n: (l, 0, n))],
        out_specs=pl.BlockSpec((1, rows, tn), lambda l, n: (l, 0, n)),
        out_shape=jax.ShapeDtypeStruct((n_layer, rows, d3), F32),
        compiler_params=_cparams("arbitrary", "arbitrary"),
        name="adaln_mod",
    )(cc, w_ada, b_ada.reshape(n_layer, 1, d3))


def _hnorm_kernel(x_ref, mod_ref, g_ref, h_ref):
    d = x_ref.shape[2]
    x = x_ref[0]
    y = x * lax.rsqrt(jnp.mean(x * x, axis=-1, keepdims=True) + EPS) * g_ref[...]
    shift = mod_ref[0, :, 0:d]
    scale = mod_ref[0, :, d:2 * d]
    h_ref[0] = (y * (1.0 + scale) + shift).astype(BF16)


def _hnorm(x, mod, g, per_batch):
    nb, seq, d = x.shape
    tm = min(seq, 1024)
    mod_map = (lambda b, i: (b, 0, 0)) if per_batch else (lambda b, i: (0, 0, 0))
    return pl.pallas_call(
        _hnorm_kernel,
        grid=(nb, seq // tm),
        in_specs=[pl.BlockSpec((1, tm, d), lambda b, i: (b, i, 0)),
                  pl.BlockSpec((1, 1, 3 * d), mod_map),
                  pl.BlockSpec((1, d), lambda b, i: (0, 0))],
        out_specs=pl.BlockSpec((1, tm, d), lambda b, i: (b, i, 0)),
        out_shape=jax.ShapeDtypeStruct((nb, seq, d), BF16),
        compiler_params=_cparams("arbitrary", "arbitrary"),
        name="mod_rmsnorm",
    )(x, mod, g.reshape(1, d))


def _conv_kernel(h_ref, w_ref, cw_ref, cb_ref, lg_ref, lb_ref, o_ref, hcv_ref, *, seq, chunk):
    zeros = jnp.zeros((CONV_PAD, W_A), F32)
    hcv_ref[0:CONV_PAD, :] = zeros
    hcv_ref[seq + CONV_PAD:seq + 2 * CONV_PAD, :] = zeros

    def glu(c, carry):
        r0 = pl.multiple_of(c * chunk, chunk)
        ag = _dot(h_ref[0, pl.ds(r0, chunk), :], w_ref[:, 0:2 * W_A])
        hcv_ref[pl.ds(r0 + CONV_PAD, chunk), :] = ag[:, :W_A] * _sigmoid(ag[:, W_A:])
        return carry

    lax.fori_loop(0, seq // chunk, glu, 0)

    def conv(c, carry):
        r0 = pl.multiple_of(c * chunk, chunk)
        acc = jnp.zeros((chunk, W_A), F32) + cb_ref[...]
        win = hcv_ref[pl.ds(r0, chunk + 2 * CONV_PAD), :]
        n_win = chunk + 2 * CONV_PAD
        for b in range(8):
            offs = [j + CONV_PAD - CONV_K // 2 for j in range(CONV_K) if (j + CONV_PAD - CONV_K // 2) % 8 == b]
            shifted = win if b == 0 else pltpu.roll(win, n_win - b, 0)
            for off in offs:
                j = off - CONV_PAD + CONV_K // 2
                acc = acc + cw_ref[j:j + 1, :] * shifted[off - b:off - b + chunk, :]
        mu = jnp.mean(acc, axis=-1, keepdims=True)
        cen = acc - mu
        var = jnp.mean(cen * cen, axis=-1, keepdims=True)
        y = _silu(cen * lax.rsqrt(var + EPS) * lg_ref[...] + lb_ref[...])
        gate = _dot(h_ref[0, pl.ds(r0, chunk), :], w_ref[:, 2 * W_A:3 * W_A])
        o_ref[0, pl.ds(r0, chunk), :] = (y * _silu(gate)).astype(BF16)
        return carry

    lax.fori_loop(0, seq // chunk, conv, 0)


def _conv_mixer(h, w, conv_w, conv_b, ln_g, ln_b):
    nb, seq, d = h.shape
    chunk = min(seq, 128)
    cw = jnp.concatenate([conv_w, jnp.zeros((32 - CONV_K, W_A), F32)], axis=0)
    return pl.pallas_call(
        functools.partial(_conv_kernel, seq=seq, chunk=chunk),
        grid=(nb,),
        in_specs=[pl.BlockSpec((1, seq, d), lambda b: (b, 0, 0)),
                  _const_spec((d, 3 * W_A)),
                  _const_spec((32, W_A)),
                  _const_spec((1, W_A)), _const_spec((1, W_A)), _const_spec((1, W_A))],
        out_specs=pl.BlockSpec((1, seq, W_A), lambda b: (b, 0, 0)),
        out_shape=jax.ShapeDtypeStruct((nb, seq, W_A), BF16),
        scratch_shapes=[pltpu.VMEM((seq + 2 * CONV_PAD, W_A), F32)],
        compiler_params=_cparams("arbitrary"),
        name="conv_mixer",
    )(h, w, cw, conv_b.reshape(1, W_A), ln_g.reshape(1, W_A), ln_b.reshape(1, W_A))


def _stage_k(src_ref, n_rows, row0, wk_ref, gk, bd, k_scr, rope_tabs):
    kc = min(n_rows, 512)
    for c in range(n_rows // kc):
        rs = slice(c * kc, (c + 1) * kc)
        k = _head_rms(_dot(src_ref[0, rs, :], wk_ref[...]), bd) * gk
        if rope_tabs is not None:
            k = _rope(k, rope_tabs[0][rs, :], rope_tabs[1][rs, :])
        dst = slice(row0 + c * kc, row0 + (c + 1) * kc)
        for n in range(k.shape[1] // HEAD_DIM):
            k_scr[n, dst, :] = k[:, n * HEAD_DIM:(n + 1) * HEAD_DIM].astype(BF16)


def _stage_vt(src_ref, n_rows, row0, wvt_ref, vt_scr):
    kc = min(n_rows, 512)
    for c in range(n_rows // kc):
        vt = _dot_nt(wvt_ref[...], src_ref[0, c * kc:(c + 1) * kc, :])
        vt_scr[:, row0 + c * kc:row0 + (c + 1) * kc] = vt.astype(BF16)


def _q_heads_t(hh, wqt_ref, gqt_ref, rope_t):
    qt = _dot_nt(wqt_ref[...], hh)
    tq = qt.shape[1]
    x = qt.reshape(qt.shape[0] // HEAD_DIM, HEAD_DIM, tq)
    x = x * lax.rsqrt(jnp.mean(x * x, axis=1, keepdims=True) + EPS)
    x = x * (_lane_tile(gqt_ref[...], tq) * (SM_SCALE * LOG2E))[None]
    if rope_t is not None:
        swapped = jnp.concatenate([x[:, 16:32], x[:, 0:16], x[:, 48:64], x[:, 32:48]], axis=1)
        x = x * rope_t[0][None] + swapped * rope_t[1][None]
    return x.astype(BF16)


def _attend(tasks, s_scr, p_scr, n_keys):
    kt = ATTN_KEY_TILE if n_keys % ATTN_KEY_TILE == 0 else LANES
    w = tasks[0][1].shape[1]
    results = []
    m_prev = None
    for step in range(len(tasks) + 1):
        scoring = step < len(tasks)
        reducing = step >= 1
        if scoring:
            k_fn, q_blk, _ = tasks[step]
            m_run = jnp.full((8, w), NEG_INF, F32)
        if reducing:
            v_fn = tasks[step - 1][2]
            l_run = jnp.zeros((8, w), F32)
        for t in range(n_keys // kt):
            rows = slice(t * kt, (t + 1) * kt)
            if scoring:
                s = _dot(k_fn(rows), q_blk)
                s_scr[step % 2, rows, :] = s
                m_run = jnp.maximum(m_run, jnp.max(s.reshape(kt // 8, 8, w), axis=0))
            if reducing:
                p = jnp.exp2(s_scr[(step - 1) % 2, rows, :] - m_prev)
                l_run = l_run + jnp.sum(p.reshape(kt // 8, 8, w), axis=0)
                p_scr[(step - 1) % 2, rows, :] = p.astype(BF16)
        if reducing:
            acc = _dot(v_fn(slice(0, n_keys)), p_scr[(step - 1) % 2])
            results.append((acc, jnp.sum(l_run, axis=0, keepdims=True)))
        if scoring:
            m_prev = jnp.max(m_run, axis=0, keepdims=True)
    return results


def _attend_whole(tasks, n_keys):
    keys = slice(0, n_keys)
    results = []
    ahead = _dot(tasks[0][0](keys), tasks[0][1])
    for i, (_, _, v_fn) in enumerate(tasks):
        s = ahead
        if i + 1 < len(tasks):
            ahead = _dot(tasks[i + 1][0](keys), tasks[i + 1][1])
        p = jnp.exp2(s - jnp.max(s, axis=0, keepdims=True))
        results.append((_dot(v_fn(keys), p.astype(BF16)), jnp.sum(p, axis=0, keepdims=True)))
    return results


def _attn_args(h, hc, wq, wk, wv, wg, gq, gk, bd):
    nb, n_ctx, d = hc.shape
    n_lat = 0 if h is None else h.shape[1]
    full = lambda b, i: (b, 0, 0)
    args, specs = [], []
    if n_lat:
        args.append(h)
        specs.append(pl.BlockSpec((1, n_lat, d), full))
    args.append(hc)
    specs.append(pl.BlockSpec((1, n_ctx, d), full))
    kw = wk.shape[1]
    gqt = jnp.broadcast_to(gq[:, None], (HEAD_DIM, LANES))
    consts = [wq.T, wk, wv.T, wg, gqt, jnp.tile(gk, kw // HEAD_DIM).reshape(1, kw), bd]
    return args + consts, specs + [_const_spec(a.shape) for a in consts], n_lat, n_ctx


def _rope_args(rope, tq):
    (cos_k, sin_k), (cos_t, sin_t) = rope
    tile = pl.BlockSpec((HEAD_DIM, tq), lambda b, i: (0, i))
    return [cos_k, sin_k, cos_t, sin_t], [_const_spec(cos_k.shape), _const_spec(sin_k.shape), tile, tile]


def _gqa_kernel(*refs, n_lat, n_ctx, n_q, n_kv, tq, per_task, rope):
    refs = list(refs)
    h_ref = refs.pop(0) if n_lat else None
    hc_ref = refs.pop(0)
    wqt_ref, wk_ref, wvt_ref, wg_ref, gqt_ref, gk_ref, bd_ref = refs[:7]
    refs = refs[7:]
    cosk_ref = sink_ref = cost_ref = sint_ref = None
    if rope:
        cosk_ref, sink_ref, cost_ref, sint_ref = refs[:4]
        refs = refs[4:]
    o_ref, k_scr, vt_scr, s_scr, p_scr = refs
    q_src = h_ref if n_lat else hc_ref
    group = n_q // n_kv
    i = pl.program_id(1)

    @pl.when(i == 0)
    def _():
        if n_lat:
            _stage_k(h_ref, n_lat, 0, wk_ref, gk_ref[...], bd_ref[...], k_scr, (cosk_ref, sink_ref) if rope else None)
            _stage_vt(h_ref, n_lat, 0, wvt_ref, vt_scr)
        _stage_k(hc_ref, n_ctx, n_lat, wk_ref, gk_ref[...], bd_ref[...], k_scr, None)
        _stage_vt(hc_ref, n_ctx, n_lat, wvt_ref, vt_scr)

    r0 = pl.multiple_of(i * tq, tq)
    hh = q_src[0, pl.ds(r0, tq), :]
    qh = _q_heads_t(hh, wqt_ref, gqt_ref, (cost_ref[...], sint_ref[...]) if rope else None)
    gate = _dot(hh, wg_ref[...])

    tasks = []
    for n in range(n_kv):
        for first in range(0, group, per_task):
            heads = [qh[n * group + first + g] for g in range(per_task)]
            q_blk = heads[0] if per_task == 1 else jnp.concatenate(heads, axis=1)
            tasks.append((lambda rows, n=n: k_scr[n, rows, :], q_blk,
                          lambda rows, n=n: vt_scr[n * HEAD_DIM:(n + 1) * HEAD_DIM, rows]))
    blocks = []
    for acc, l in _attend(tasks, s_scr, p_scr, n_lat + n_ctx):
        ot = acc / l
        blocks.extend(ot[:, g * tq:(g + 1) * tq] for g in range(per_task))
    yt = jnp.concatenate(blocks, axis=0)
    o_ref[0] = (yt.T * _silu(gate)).astype(BF16)


def _gqa(h, hc, wq, wk, wv, wg, gq, gk, bd, rope, n_q, n_kv, tq):
    args, specs, n_lat, n_ctx = _attn_args(h, hc, wq, wk, wv, wg, gq, gk, bd)
    nb = hc.shape[0]
    sq = n_lat if n_lat else n_ctx
    tq = min(tq, sq)
    qw, kw = n_q * HEAD_DIM, n_kv * HEAD_DIM
    per_task = max(1, min(n_q // n_kv, ATTN_LANES // tq))
    if rope is not None:
        a, s = _rope_args(rope, tq)
        args += a
        specs += s
    return pl.pallas_call(
        functools.partial(_gqa_kernel, n_lat=n_lat, n_ctx=n_ctx, n_q=n_q, n_kv=n_kv, tq=tq, per_task=per_task,
                          rope=rope is not None),
        grid=(nb, sq // tq),
        in_specs=specs,
        out_specs=pl.BlockSpec((1, tq, qw), lambda b, i: (b, i, 0)),
        out_shape=jax.ShapeDtypeStruct((nb, sq, qw), BF16),
        scratch_shapes=[pltpu.VMEM((n_kv, n_lat + n_ctx, HEAD_DIM), BF16),
                        pltpu.VMEM((kw, n_lat + n_ctx), BF16),
                        pltpu.VMEM((2, n_lat + n_ctx, per_task * tq), F32),
                        pltpu.VMEM((2, n_lat + n_ctx, per_task * tq), BF16)],
        compiler_params=_cparams("arbitrary", "arbitrary"),
        name="gqa_lat" if n_lat else "gqa_ctx",
    )(*args)


def _diff_kernel(*refs, n_lat, n_ctx, tq, rope, lam_init):
    refs = list(refs)
    h_ref = refs.pop(0) if n_lat else None
    hc_ref = refs.pop(0)
    wqt_ref, wk_ref, wvt_ref, wg_ref, gqt_ref, gk_ref, bd_ref, lam_ref, sgt_ref = refs[:9]
    refs = refs[9:]
    cosk_ref = sink_ref = cost_ref = sint_ref = None
    if rope:
        cosk_ref, sink_ref, cost_ref, sint_ref = refs[:4]
        refs = refs[4:]
    o_ref, k_scr, vt_scr = refs
    q_src = h_ref if n_lat else hc_ref
    i = pl.program_id(1)

    @pl.when(i == 0)
    def _():
        if n_lat:
            _stage_k(h_ref, n_lat, 0, wk_ref, gk_ref[...], bd_ref[...], k_scr, (cosk_ref, sink_ref) if rope else None)
            _stage_vt(h_ref, n_lat, 0, wvt_ref, vt_scr)
        _stage_k(hc_ref, n_ctx, n_lat, wk_ref, gk_ref[...], bd_ref[...], k_scr, None)
        _stage_vt(hc_ref, n_ctx, n_lat, wvt_ref, vt_scr)

    lam_p = lam_ref[...]
    lam = (jnp.exp(jnp.sum(lam_p[0:1] * lam_p[1:2], axis=-1, keepdims=True))
           - jnp.exp(jnp.sum(lam_p[2:3] * lam_p[3:4], axis=-1, keepdims=True)) + lam_init)

    r0 = pl.multiple_of(i * tq, tq)
    hh = q_src[0, pl.ds(r0, tq), :]
    qh = _q_heads_t(hh, wqt_ref, gqt_ref, (cost_ref[...], sint_ref[...]) if rope else None)
    gate = _dot(hh, wg_ref[...])
    sub_gain = _lane_tile(sgt_ref[...], tq) * (1.0 - lam_init)

    tasks = []
    for sub in range(2 * H_D):
        v_rows = slice((sub // 2) * 2 * HEAD_DIM, (sub // 2 + 1) * 2 * HEAD_DIM)
        tasks.append((lambda rows, sub=sub: k_scr[sub, rows, :], qh[sub],
                      lambda rows, v_rows=v_rows: vt_scr[v_rows, rows]))
    res = _attend_whole(tasks, n_lat + n_ctx)
    blocks = []
    for hd in range(H_D):
        (pv0, l0), (pv1, l1) = res[2 * hd], res[2 * hd + 1]
        o = pv0 / l0 - lam * (pv1 / l1)
        blocks.append(o * lax.rsqrt(jnp.mean(o * o, axis=0, keepdims=True) + EPS) * sub_gain)
    yt = jnp.concatenate(blocks, axis=0)
    o_ref[0] = (yt.T * _silu(gate)).astype(BF16)


def _diff_attn(h, hc, wq, wk, wv, wg, gq, gk, bd, lam_p, subln_g, rope, lam_init, tq):
    args, specs, n_lat, n_ctx = _attn_args(h, hc, wq, wk, wv, wg, gq, gk, bd)
    nb = hc.shape[0]
    sq = n_lat if n_lat else n_ctx
    tq = min(tq, sq)
    sgt = jnp.broadcast_to(subln_g[:, None], (2 * HEAD_DIM, LANES))
    args += [lam_p, sgt]
    specs += [_const_spec(lam_p.shape), _const_spec(sgt.shape)]
    if rope is not None:
        a, s = _rope_args(rope, tq)
        args += a
        specs += s
    return pl.pallas_call(
        functools.partial(_diff_kernel, n_lat=n_lat, n_ctx=n_ctx, tq=tq, rope=rope is not None, lam_init=lam_init),
        grid=(nb, sq // tq),
        in_specs=specs,
        out_specs=pl.BlockSpec((1, tq, W_D), lambda b, i: (b, i, 0)),
        out_shape=jax.ShapeDtypeStruct((nb, sq, W_D), BF16),
        scratch_shapes=[pltpu.VMEM((2 * H_D, n_lat + n_ctx, HEAD_DIM), BF16),
                        pltpu.VMEM((W_D, n_lat + n_ctx), BF16)],
        compiler_params=_cparams("arbitrary", "arbitrary"),
        name="diff_lat" if n_lat else "diff_ctx",
    )(*args)


NA_TILE = 4
NA_BAND = NA_TILE + NA_ROWS
NA_VROWS = NA_ROWS // 2
NA_VBLK = NA_VROWS * GRID_W


def _rpb_table_kernel(rpb_ref, onehot_ref, mask_ref, o_ref):
    r = rpb_ref[...]
    hi = r.astype(BF16)
    mid = (r - hi.astype(F32)).astype(BF16)
    lo = (r - hi.astype(F32) - mid.astype(F32)).astype(BF16)
    oh = onehot_ref[...]
    o_ref[...] = (_dot(hi, oh) + _dot(mid, oh) + _dot(lo, oh) + mask_ref[...]) * LOG2E


def _rpb_tables(na_rpb):
    n_layer = na_rpb.shape[0]
    n_dr, n_dc = 2 * NA_ROWS - 1, 2 * NA_COLS - 1
    cq = np.arange(GRID_W)
    c0 = np.clip(cq - NA_COLS // 2, 0, GRID_W - NA_COLS)
    col_ok = (cq[None, :] >= c0[:, None]) & (cq[None, :] < c0[:, None] + NA_COLS)
    dc = np.clip(cq[None, :] - cq[:, None] + (NA_COLS - 1), 0, n_dc - 1)
    onehot = (np.arange(32)[:, None, None] == dc.T[None]) & col_ok.T[None]
    onehot = jnp.asarray(onehot.reshape(32, GRID_W * GRID_W), BF16)
    mask = jnp.asarray(np.where(col_ok.T, 0.0, NEG_INF).reshape(1, GRID_W * GRID_W), F32)
    rows = n_layer * H_B * n_dr
    rows_pad = -(-rows // 8) * 8
    rpb2 = jnp.zeros((rows_pad, 32), F32).at[:rows, :n_dc].set(na_rpb.reshape(rows, n_dc))
    tab = pl.pallas_call(
        _rpb_table_kernel,
        out_shape=jax.ShapeDtypeStruct((rows_pad, GRID_W * GRID_W), F32),
        compiler_params=pltpu.CompilerParams(vmem_limit_bytes=VMEM_LIMIT),
        name="rpb_table",
    )(rpb2, onehot, mask)
    tab = tab[:rows].reshape(n_layer, H_B, n_dr, GRID_W, GRID_W)
    zero = jnp.zeros_like(tab[:, :, :1])
    padded = jnp.concatenate([zero, tab, zero], axis=2)
    return jnp.concatenate([padded[:, :, 1:], padded[:, :, :-1]], axis=-1)


def _natten_kernel(h_ref, hc_ref, wqt_ref, wk_ref, wvt_ref, wg_ref, gqt_ref, gk_ref, bd_ref, tab_ref,
                   o_ref, k_scr, vt_scr, vtc_scr, *, n_lat, n_ctx):
    t = pl.program_id(1)
    n_rows = n_lat // GRID_W
    tq = NA_TILE * GRID_W
    band = NA_BAND * GRID_W

    @pl.when(t == 0)
    def _():
        _stage_k(h_ref, n_lat, 0, wk_ref, gk_ref[...], bd_ref[...], k_scr, None)
        _stage_k(hc_ref, n_ctx, n_lat, wk_ref, gk_ref[...], bd_ref[...], k_scr, None)
        for c in range(n_lat // NA_VBLK):
            vt_scr[c] = _dot_nt(wvt_ref[...], h_ref[0, c * NA_VBLK:(c + 1) * NA_VBLK, :]).astype(BF16)
        _stage_vt(hc_ref, n_ctx, 0, wvt_ref, vtc_scr)

    blk0 = jnp.clip(t * (NA_TILE // NA_VROWS) - 1, 0, n_rows // NA_VROWS - NA_BAND // NA_VROWS)
    u0 = blk0 * NA_VROWS
    k0 = pl.multiple_of(u0 * GRID_W, NA_VBLK)

    lane_row = lax.broadcasted_iota(jnp.int32, (1, tq), 1) >> GRID_SHIFT
    mask_rows, bias_idx = [], []
    for u in range(NA_BAND):
        row = jnp.zeros((1, tq), F32)
        for i in range(NA_TILE):
            r_band = jnp.clip(t * NA_TILE + i - NA_ROWS // 2, 0, n_rows - NA_ROWS)
            ok = (u0 + u >= r_band) & (u0 + u < r_band + NA_ROWS)
            row = jnp.where(lane_row == i, jnp.where(ok, 0.0, NEG_INF), row)
        mask_rows.append(row)
        j = u0 + u - t * NA_TILE + (NA_ROWS - 1)
        bias_idx.append([jnp.clip(j - 2 * ip, 0, 2 * NA_ROWS - 1) for ip in range(NA_TILE // 2)])

    q0 = pl.multiple_of(t * tq, tq)
    hh = h_ref[0, pl.ds(q0, tq), :]
    qh = _q_heads_t(hh, wqt_ref, gqt_ref, None)
    gate = _dot(hh, wg_ref[...])

    def scores(hd):
        return (_dot(k_scr[hd, pl.ds(k0, band), :], qh[hd]),
                _dot(k_scr[hd, n_lat:n_lat + n_ctx, :], qh[hd]))

    blocks = []
    ahead = scores(0)
    for hd in range(H_B):
        st, sc = ahead
        if hd + 1 < H_B:
            ahead = scores(hd + 1)
        rows = []
        for u in range(NA_BAND):
            bias = jnp.concatenate([tab_ref[hd, jj] for jj in bias_idx[u]], axis=1)
            rows.append(st[u * GRID_W:(u + 1) * GRID_W, :] + bias + mask_rows[u])
        st = jnp.concatenate(rows, axis=0)
        m = jnp.maximum(jnp.max(st, axis=0, keepdims=True), jnp.max(sc, axis=0, keepdims=True))
        p = jnp.exp2(st - m)
        pc = jnp.exp2(sc - m)
        l = jnp.sum(p, axis=0, keepdims=True) + jnp.sum(pc, axis=0, keepdims=True)
        pb = p.astype(BF16)
        hs = slice(hd * HEAD_DIM, (hd + 1) * HEAD_DIM)
        ot = _dot(vtc_scr[hs, :], pc.astype(BF16))
        for c in range(NA_BAND // NA_VROWS):
            ot = ot + _dot(vt_scr[blk0 + c, hs, :], pb[c * NA_VBLK:(c + 1) * NA_VBLK, :])
        blocks.append(ot / l)
    yt = jnp.concatenate(blocks, axis=0)
    o_ref[0] = (yt.T * _silu(gate)).astype(BF16)


def _natten(h, hc, wq, wk, wv, wg, gq, gk, bd, tab):
    nb, n_lat, d = h.shape
    n_ctx = hc.shape[1]
    n_rows = n_lat // GRID_W
    assert n_rows % NA_TILE == 0 and n_rows >= NA_BAND
    tq = NA_TILE * GRID_W
    full = lambda b, t: (b, 0, 0)
    gqt = jnp.broadcast_to(gq[:, None], (HEAD_DIM, LANES))
    consts = [wq.T, wk, wv.T, wg, gqt, jnp.tile(gk, H_B).reshape(1, W_B), bd, tab]
    return pl.pallas_call(
        functools.partial(_natten_kernel, n_lat=n_lat, n_ctx=n_ctx),
        grid=(nb, n_lat // tq),
        in_specs=[pl.BlockSpec((1, n_lat, d), full), pl.BlockSpec((1, n_ctx, d), full)]
                 + [_const_spec(a.shape) for a in consts],
        out_specs=pl.BlockSpec((1, tq, W_B), lambda b, t: (b, t, 0)),
        out_shape=jax.ShapeDtypeStruct((nb, n_lat, W_B), BF16),
        scratch_shapes=[pltpu.VMEM((H_B, n_lat + n_ctx, HEAD_DIM), BF16),
                        pltpu.VMEM((n_lat // NA_VBLK, W_B, NA_VBLK), BF16),
                        pltpu.VMEM((W_B, n_ctx), BF16)],
        compiler_params=_cparams("arbitrary", "arbitrary"),
        name="natten",
    )(h, hc, *consts)


def _merge_kernel(h_ref, ya_ref, yb_ref, yc_ref, yd_ref, wl_ref, bm_ref, wbr_ref, wo_ref, x_ref, mod_ref, o_ref):
    d = x_ref.shape[2]
    hh = h_ref[0]
    m = None
    for br, y_ref in enumerate((ya_ref, yb_ref, yc_ref, yd_ref)):
        cols = slice(br * d, (br + 1) * d)
        g = _sigmoid(_dot(hh, wl_ref[:, cols]) + bm_ref[:, cols])
        t = g * _dot(y_ref[0], wbr_ref[br])
        m = t if m is None else m + t
    gate = mod_ref[0, :, 2 * d:3 * d]
    o_ref[0] = x_ref[0] + gate * _dot(m.astype(BF16), wo_ref[...])


def _merge(h, ys, wl, b_merge, wbr, wo, x, mod, per_batch):
    nb, seq, d = x.shape
    tm = min(seq, 512)
    tile = lambda b, i: (b, i, 0)
    mod_map = (lambda b, i: (b, 0, 0)) if per_batch else (lambda b, i: (0, 0, 0))
    return pl.pallas_call(
        _merge_kernel,
        grid=(nb, seq // tm),
        in_specs=[pl.BlockSpec((1, tm, d), tile)]
                 + [pl.BlockSpec((1, tm, W_A), tile)] * N_BRANCH
                 + [_const_spec(wl.shape), _const_spec((1, N_BRANCH * d)), _const_spec(wbr.shape), _const_spec(wo.shape),
                    pl.BlockSpec((1, tm, d), tile),
                    pl.BlockSpec((1, 1, 3 * d), mod_map)],
        out_specs=pl.BlockSpec((1, tm, d), tile),
        out_shape=jax.ShapeDtypeStruct((nb, seq, d), F32),
        compiler_params=_cparams("arbitrary", "arbitrary"),
        name="merge",
    )(h, *ys, wl, b_merge.reshape(1, N_BRANCH * d), wbr, wo, x, mod)


def _rope_tables(seq):
    half = HEAD_DIM // 2
    t = np.arange(seq)
    freqs = ROPE_THETA ** (-np.arange(0, half, 2, dtype=np.float64) / half)
    blocks_c, blocks_s = [], []
    for pos in (t // GRID_W, t % GRID_W):
        ang = pos[:, None].astype(np.float64) * freqs[None, :]
        blocks_c += [np.cos(ang), np.cos(ang)]
        blocks_s += [-np.sin(ang), np.sin(ang)]
    cos = np.concatenate(blocks_c, axis=1)
    sin = np.concatenate(blocks_s, axis=1)
    reps = LANES // HEAD_DIM
    key_side = (jnp.asarray(np.tile(cos, (1, reps)), F32), jnp.asarray(np.tile(sin, (1, reps)), F32))
    query_side = (jnp.asarray(cos.T, F32), jnp.asarray(sin.T, F32))
    return key_side, query_side


def _group_mean_matrix():
    idx = np.arange(LANES) // HEAD_DIM
    return jnp.asarray((idx[:, None] == idx[None, :]) / HEAD_DIM, BF16)


def _qkvg(w, off, qw, kw, vw, gw):
    edges = np.cumsum([off, qw, kw, vw, gw])
    return tuple(w[:, a:b] for a, b in zip(edges[:-1], edges[1:]))


def kernel(x, c, ctx, c_ctx, w_ada, b_ada, norm_g, w_in, b_merge, conv_w, conv_b, conv_ln_g, conv_ln_b, na_qn_g, na_kn_g, na_rpb, gqa_qn_g, gqa_kn_g, diff_qn_g, diff_kn_g, lam_q1, lam_k1, lam_q2, lam_k2, diff_subln_g, w_br_a, w_br_b, w_br_c, w_br_d, w_out):
    n_batch, seq, d = x.shape
    depth = w_in.shape[0]
    rope = _rope_tables(seq)
    bd = _group_mean_matrix()
    tabs = _rpb_tables(na_rpb)

    rows = -(-(n_batch + 1) // 8) * 8
    cc = jnp.zeros((rows, d), F32).at[:n_batch].set(c).at[n_batch].set(c_ctx)
    mod_all = _modulation(cc, w_ada, b_ada)

    xc = ctx
    for l in range(depth):
        need_ctx = l < depth - 1
        lam_init = 0.8 - 0.6 * math.exp(-0.3 * l)
        mod_x = mod_all[l, :n_batch, None, :]
        mod_c = mod_all[l, n_batch:n_batch + 1, None, :]
        h = _hnorm(x, mod_x, norm_g[l], True)
        hc = _hnorm(xc, mod_c, norm_g[l], False)
        w = w_in[l].astype(BF16)
        w_a = w[:, OFF_A:OFF_B]
        w_b = _qkvg(w, OFF_B, W_B, W_B, W_B, W_B)
        w_c = _qkvg(w, OFF_C, W_C, W_C_KV, W_C_KV, W_C)
        w_d = _qkvg(w, OFF_D, W_D, W_D, W_D, W_D)
        w_l = w[:, OFF_L:]
        wbr = jnp.stack([w_br_a[l], w_br_b[l], w_br_c[l], w_br_d[l]]).astype(BF16)
        wo = w_out[l].astype(BF16)
        lam_p = jnp.stack([lam_q1[l], lam_k1[l], lam_q2[l], lam_k2[l]])

        ya = _conv_mixer(h, w_a, conv_w[l], conv_b[l], conv_ln_g[l], conv_ln_b[l])
        yb = _natten(h, hc, *w_b, na_qn_g[l], na_kn_g[l], bd, tabs[l])
        yc = _gqa(h, hc, *w_c, gqa_qn_g[l], gqa_kn_g[l], bd, rope, H_C, KV_C, 512)
        yd = _diff_attn(h, hc, *w_d, diff_qn_g[l], diff_kn_g[l], bd, lam_p, diff_subln_g[l], rope, lam_init, 512)
        x_new = _merge(h, (ya, yb, yc, yd), w_l, b_merge[l], wbr, wo, x, mod_x, True)

        if need_ctx:
            ca = _conv_mixer(hc, w_a, conv_w[l], conv_b[l], conv_ln_g[l], conv_ln_b[l])
            cb = _gqa(None, hc, *w_b, na_qn_g[l], na_kn_g[l], bd, None, H_B, H_B, 256)
            cg = _gqa(None, hc, *w_c, gqa_qn_g[l], gqa_kn_g[l], bd, None, H_C, KV_C, 256)
            cd = _diff_attn(None, hc, *w_d, diff_qn_g[l], diff_kn_g[l], bd, lam_p, diff_subln_g[l], None, lam_init, 256)
            xc = _merge(hc, (ca, cb, cg, cd), w_l, b_merge[l], wbr, wo, xc, mod_c, False)
        x = x_new
    return x
```

```python
import functools
import math

import numpy as np
import jax
import jax.numpy as jnp
from jax import lax
from jax.experimental import pallas as pl
from jax.experimental.pallas import tpu as pltpu

F32 = jnp.float32
BF16 = jnp.bfloat16

GRID_W = 64
GRID_SHIFT = GRID_W.bit_length() - 1
HEAD_DIM = 64
N_BRANCH = 4
W_A = 512
CONV_K = 31
CONV_PAD = 16
H_B = 8
W_B = H_B * HEAD_DIM
NA_ROWS = 8
NA_COLS = 16
H_C = 8
KV_C = 2
W_C = H_C * HEAD_DIM
W_C_KV = KV_C * HEAD_DIM
H_D = 4
W_D = H_D * 2 * HEAD_DIM
ROPE_THETA = 10000.0
EPS = 1e-6
NEG_INF = -1e30
LOG2E = math.log2(math.e)
SM_SCALE = HEAD_DIM ** -0.5

OFF_A = 0
OFF_B = OFF_A + 3 * W_A
OFF_C = OFF_B + 4 * W_B
OFF_D = OFF_C + 2 * W_C + 2 * W_C_KV
OFF_L = OFF_D + 4 * W_D
LANES = 128
ATTN_LANES = 512
ATTN_KEY_TILE = 256
VMEM_LIMIT = 56 * 1024 * 1024


def _cparams(*sem):
    return pltpu.CompilerParams(dimension_semantics=sem, vmem_limit_bytes=VMEM_LIMIT)


def _const_spec(shape):
    nd = len(shape)
    return pl.BlockSpec(shape, lambda *_: (0,) * nd, pipeline_mode=pl.Buffered(1))


def _sigmoid(x):
    return 1.0 / (1.0 + jnp.exp(-x))


def _silu(x):
    return x * _sigmoid(x)


def _dot(a, b):
    return jnp.dot(a, b, preferred_element_type=F32)


def _dot_nt(a, b):
    return lax.dot_general(a, b, (((1,), (1,)), ((), ())), preferred_element_type=F32)


def _split_bf16(x):
    hi = x.astype(BF16)
    lo = (x - hi.astype(F32)).astype(BF16)
    return hi, lo


def _head_rms(x, bd):
    x2 = x * x
    hi, lo = _split_bf16(x2)
    cols = []
    for c in range(x.shape[1] // LANES):
        sl = slice(c * LANES, (c + 1) * LANES)
        cols.append(_dot(hi[:, sl], bd) + _dot(lo[:, sl], bd))
    ms = cols[0] if len(cols) == 1 else jnp.concatenate(cols, axis=1)
    return x * lax.rsqrt(ms + EPS)


def _lane_tile(t, width):
    reps = width // t.shape[1]
    return t if reps == 1 else jnp.concatenate([t] * reps, axis=1)


def _rope(x, cos, sin):
    n = x.shape[1]
    fwd = pltpu.roll(x, n - 16, 1)
    bwd = pltpu.roll(x, 16, 1)
    lane = lax.broadcasted_iota(jnp.int32, x.shape, 1)
    swapped = jnp.where((lane & 16) == 0, fwd, bwd)
    return x * _lane_tile(cos, n) + swapped * _lane_tile(sin, n)


def _mod_kernel(cc_ref, w_ref, b_ref, o_ref):
    a = _silu(cc_ref[...])
    a_hi, a_lo = _split_bf16(a)
    w_hi, w_lo = _split_bf16(w_ref[0])
    o_ref[0] = _dot(a_hi, w_hi) + _dot(a_hi, w_lo) + _dot(a_lo, w_hi) + b_ref[0]


def _modulation(cc, w_ada, b_ada):
    n_layer, d, d3 = w_ada.shape
    rows = cc.shape[0]
    tn = 768
    return pl.pallas_call(
        _mod_kernel,
        grid=(n_layer, d3 // tn),
        in_specs=[pl.BlockSpec((rows, d), lambda l, n: (0, 0)),
                  pl.BlockSpec((1, d, tn), lambda l, n: (l, 0, n)),
                  pl.BlockSpec((1, 1, tn), lambda l, n: (l, 0, n))],
        out_specs=pl.BlockSpec((1, rows, tn), lambda l, n: (l, 0, n)),
        out_shape=jax.ShapeDtypeStruct((n_layer, rows, d3), F32),
        compiler_params=_cparams("arbitrary", "arbitrary"),
        name="adaln_mod",
    )(cc, w_ada, b_ada.reshape(n_layer, 1, d3))


def _hnorm_kernel(x_ref, mod_ref, g_ref, h_ref):
    d = x_ref.shape[2]
    x = x_ref[0]
    y = x * lax.rsqrt(jnp.mean(x * x, axis=-1, keepdims=True) + EPS) * g_ref[...]
    shift = mod_ref[0, :, 0:d]
    scale = mod_ref[0, :, d:2 * d]
    h_ref[0] = (y * (1.0 + scale) + shift).astype(BF16)


def _hnorm(x, mod, g, per_batch):
    nb, seq, d = x.shape
    tm = min(seq, 1024)
    mod_map = (lambda b, i: (b, 0, 0)) if per_batch else (lambda b, i: (0, 0, 0))
    return pl.pallas_call(
        _hnorm_kernel,
        grid=(nb, seq // tm),
        in_specs=[pl.BlockSpec((1, tm, d), lambda b, i: (b, i, 0)),
                  pl.BlockSpec((1, 1, 3 * d), mod_map),
                  pl.BlockSpec((1, d), lambda b, i: (0, 0))],
        out_specs=pl.BlockSpec((1, tm, d), lambda b, i: (b, i, 0)),
        out_shape=jax.ShapeDtypeStruct((nb, seq, d), BF16),
        compiler_params=_cparams("arbitrary", "arbitrary"),
        name="mod_rmsnorm",
    )(x, mod, g.reshape(1, d))


def _conv_kernel(h_ref, w_ref, cw_ref, cb_ref, lg_ref, lb_ref, o_ref, hcv_ref, *, seq, chunk):
    zeros = jnp.zeros((CONV_PAD, W_A), F32)
    hcv_ref[0:CONV_PAD, :] = zeros
    hcv_ref[seq + CONV_PAD:seq + 2 * CONV_PAD, :] = zeros

    glu_rows = min(seq, 2 * chunk)

    def glu(c, carry):
        r0 = pl.multiple_of(c * glu_rows, glu_rows)
        ag = _dot(h_ref[0, pl.ds(r0, glu_rows), :], w_ref[:, 0:2 * W_A])
        hcv_ref[pl.ds(r0 + CONV_PAD, glu_rows), :] = ag[:, :W_A] * _sigmoid(ag[:, W_A:])
        return carry

    lax.fori_loop(0, seq // glu_rows, glu, 0)

    def conv(c, carry):
        r0 = pl.multiple_of(c * chunk, chunk)
        acc = jnp.zeros((chunk, W_A), F32) + cb_ref[...]
        win = hcv_ref[pl.ds(r0, chunk + 2 * CONV_PAD), :]
        n_win = chunk + 2 * CONV_PAD
        for b in range(8):
            offs = [j + CONV_PAD - CONV_K // 2 for j in range(CONV_K) if (j + CONV_PAD - CONV_K // 2) % 8 == b]
            shifted = win if b == 0 else pltpu.roll(win, n_win - b, 0)
            for off in offs:
                j = off - CONV_PAD + CONV_K // 2
                acc = acc + cw_ref[j:j + 1, :] * shifted[off - b:off - b + chunk, :]
        mu = jnp.mean(acc, axis=-1, keepdims=True)
        cen = acc - mu
        var = jnp.mean(cen * cen, axis=-1, keepdims=True)
        y = _silu(cen * lax.rsqrt(var + EPS) * lg_ref[...] + lb_ref[...])
        gate = _dot(h_ref[0, pl.ds(r0, chunk), :], w_ref[:, 2 * W_A:3 * W_A])
        o_ref[0, pl.ds(r0, chunk), :] = (y * _silu(gate)).astype(BF16)
        return carry

    lax.fori_loop(0, seq // chunk, conv, 0)


def _conv_mixer(h, w, conv_w, conv_b, ln_g, ln_b):
    nb, seq, d = h.shape
    chunk = min(seq, 128)
    cw = jnp.concatenate([conv_w, jnp.zeros((32 - CONV_K, W_A), F32)], axis=0)
    return pl.pallas_call(
        functools.partial(_conv_kernel, seq=seq, chunk=chunk),
        grid=(nb,),
        in_specs=[pl.BlockSpec((1, seq, d), lambda b: (b, 0, 0)),
                  _const_spec((d, 3 * W_A)),
                  _const_spec((32, W_A)),
                  _const_spec((1, W_A)), _const_spec((1, W_A)), _const_spec((1, W_A))],
        out_specs=pl.BlockSpec((1, seq, W_A), lambda b: (b, 0, 0)),
        out_shape=jax.ShapeDtypeStruct((nb, seq, W_A), BF16),
        scratch_shapes=[pltpu.VMEM((seq + 2 * CONV_PAD, W_A), F32)],
        compiler_params=_cparams("arbitrary"),
        name="conv_mixer",
    )(h, w, cw, conv_b.reshape(1, W_A), ln_g.reshape(1, W_A), ln_b.reshape(1, W_A))


def _stage_k(src_ref, n_rows, row0, wk_ref, gk, bd, k_scr, rope_tabs):
    kc = min(n_rows, 512)
    for c in range(n_rows // kc):
        rs = slice(c * kc, (c + 1) * kc)
        k = _head_rms(_dot(src_ref[0, rs, :], wk_ref[...]), bd) * gk
        if rope_tabs is not None:
            k = _rope(k, rope_tabs[0][rs, :], rope_tabs[1][rs, :])
        dst = slice(row0 + c * kc, row0 + (c + 1) * kc)
        for n in range(k.shape[1] // HEAD_DIM):
            k_scr[n, dst, :] = k[:, n * HEAD_DIM:(n + 1) * HEAD_DIM].astype(BF16)


def _stage_vt(src_ref, n_rows, row0, wvt_ref, vt_scr):
    kc = min(n_rows, 512)
    for c in range(n_rows // kc):
        vt = _dot_nt(wvt_ref[...], src_ref[0, c * kc:(c + 1) * kc, :])
        vt_scr[:, row0 + c * kc:row0 + (c + 1) * kc] = vt.astype(BF16)


def _q_heads_t(hh, wqt_ref, gqt_ref, rope_t):
    qt = _dot_nt(wqt_ref[...], hh)
    tq = qt.shape[1]
    x = qt.reshape(qt.shape[0] // HEAD_DIM, HEAD_DIM, tq)
    x = x * lax.rsqrt(jnp.mean(x * x, axis=1, keepdims=True) + EPS)
    x = x * (_lane_tile(gqt_ref[...], tq) * (SM_SCALE * LOG2E))[None]
    if rope_t is not None:
        swapped = jnp.concatenate([x[:, 16:32], x[:, 0:16], x[:, 48:64], x[:, 32:48]], axis=1)
        x = x * rope_t[0][None] + swapped * rope_t[1][None]
    return x.astype(BF16)


def _attend(tasks, s_scr, p_scr, n_keys):
    kt = ATTN_KEY_TILE if n_keys % ATTN_KEY_TILE == 0 else LANES
    w = tasks[0][1].shape[1]
    results = []
    m_prev = None
    for step in range(len(tasks) + 1):
        scoring = step < len(tasks)
        reducing = step >= 1
        if scoring:
            k_fn, q_blk, _ = tasks[step]
            m_run = jnp.full((8, w), NEG_INF, F32)
        if reducing:
            v_fn = tasks[step - 1][2]
            l_run = jnp.zeros((8, w), F32)
        for t in range(n_keys // kt):
            rows = slice(t * kt, (t + 1) * kt)
            if scoring:
                s = _dot(k_fn(rows), q_blk)
                s_scr[step % 2, rows, :] = s
                m_run = jnp.maximum(m_run, jnp.max(s.reshape(kt // 8, 8, w), axis=0))
            if reducing:
                p = jnp.exp2(s_scr[(step - 1) % 2, rows, :] - m_prev)
                l_run = l_run + jnp.sum(p.reshape(kt // 8, 8, w), axis=0)
                p_scr[(step - 1) % 2, rows, :] = p.astype(BF16)
        if reducing:
            acc = _dot(v_fn(slice(0, n_keys)), p_scr[(step - 1) % 2])
            results.append((acc, jnp.sum(l_run, axis=0, keepdims=True)))
        if scoring:
            m_prev = jnp.max(m_run, axis=0, keepdims=True)
    return results


def _attend_whole(tasks, n_keys):
    keys = slice(0, n_keys)
    results = []
    for k_fn, q_blk, v_fn in tasks:
        s = _dot(k_fn(keys), q_blk)
        p = jnp.exp2(s - jnp.max(s, axis=0, keepdims=True))
        results.append((_dot(v_fn(keys), p.astype(BF16)), jnp.sum(p, axis=0, keepdims=True)))
    return results


def _attn_args(h, hc, wq, wk, wv, wg, gq, gk, bd):
    nb, n_ctx, d = hc.shape
    n_lat = 0 if h is None else h.shape[1]
    full = lambda b, i: (b, 0, 0)
    args, specs = [], []
    if n_lat:
        args.append(h)
        specs.append(pl.BlockSpec((1, n_lat, d), full))
    args.append(hc)
    specs.append(pl.BlockSpec((1, n_ctx, d), full))
    kw = wk.shape[1]
    gqt = jnp.broadcast_to(gq[:, None], (HEAD_DIM, LANES))
    consts = [wq.T, wk, wv.T, wg, gqt, jnp.tile(gk, kw // HEAD_DIM).reshape(1, kw), bd]
    return args + consts, specs + [_const_spec(a.shape) for a in consts], n_lat, n_ctx


def _rope_args(rope, tq):
    (cos_k, sin_k), (cos_t, sin_t) = rope
    tile = pl.BlockSpec((HEAD_DIM, tq), lambda b, i: (0, i))
    return [cos_k, sin_k, cos_t, sin_t], [_const_spec(cos_k.shape), _const_spec(sin_k.shape), tile, tile]


def _gqa_kernel(*refs, n_lat, n_ctx, n_q, n_kv, tq, per_task, rope):
    refs = list(refs)
    h_ref = refs.pop(0) if n_lat else None
    hc_ref = refs.pop(0)
    wqt_ref, wk_ref, wvt_ref, wg_ref, gqt_ref, gk_ref, bd_ref = refs[:7]
    refs = refs[7:]
    cosk_ref = sink_ref = cost_ref = sint_ref = None
    if rope:
        cosk_ref, sink_ref, cost_ref, sint_ref = refs[:4]
        refs = refs[4:]
    o_ref, k_scr, vt_scr, s_scr, p_scr = refs
    q_src = h_ref if n_lat else hc_ref
    group = n_q // n_kv
    i = pl.program_id(1)

    @pl.when(i == 0)
    def _():
        if n_lat:
            _stage_k(h_ref, n_lat, 0, wk_ref, gk_ref[...], bd_ref[...], k_scr, (cosk_ref, sink_ref) if rope else None)
            _stage_vt(h_ref, n_lat, 0, wvt_ref, vt_scr)
        _stage_k(hc_ref, n_ctx, n_lat, wk_ref, gk_ref[...], bd_ref[...], k_scr, None)
        _stage_vt(hc_ref, n_ctx, n_lat, wvt_ref, vt_scr)

    r0 = pl.multiple_of(i * tq, tq)
    hh = q_src[0, pl.ds(r0, tq), :]
    qh = _q_heads_t(hh, wqt_ref, gqt_ref, (cost_ref[...], sint_ref[...]) if rope else None)
    gate = _dot(hh, wg_ref[...])

    tasks = []
    for n in range(n_kv):
        for first in range(0, group, per_task):
            heads = [qh[n * group + first + g] for g in range(per_task)]
            q_blk = heads[0] if per_task == 1 else jnp.concatenate(heads, axis=1)
            tasks.append((lambda rows, n=n: k_scr[n, rows, :], q_blk,
                          lambda rows, n=n: vt_scr[n * HEAD_DIM:(n + 1) * HEAD_DIM, rows]))
    blocks = []
    for acc, l in _attend(tasks, s_scr, p_scr, n_lat + n_ctx):
        ot = acc / l
        blocks.extend(ot[:, g * tq:(g + 1) * tq] for g in range(per_task))
    yt = jnp.concatenate(blocks, axis=0)
    o_ref[0] = (yt.T * _silu(gate)).astype(BF16)


def _gqa(h, hc, wq, wk, wv, wg, gq, gk, bd, rope, n_q, n_kv, tq):
    args, specs, n_lat, n_ctx = _attn_args(h, hc, wq, wk, wv, wg, gq, gk, bd)
    nb = hc.shape[0]
    sq = n_lat if n_lat else n_ctx
    tq = min(tq, sq)
    qw, kw = n_q * HEAD_DIM, n_kv * HEAD_DIM
    per_task = max(1, min(n_q // n_kv, ATTN_LANES // tq))
    if rope is not None:
        a, s = _rope_args(rope, tq)
        args += a
        specs += s
    return pl.pallas_call(
        functools.partial(_gqa_kernel, n_lat=n_lat, n_ctx=n_ctx, n_q=n_q, n_kv=n_kv, tq=tq, per_task=per_task,
                          rope=rope is not None),
        grid=(nb, sq // tq),
        in_specs=specs,
        out_specs=pl.BlockSpec((1, tq, qw), lambda b, i: (b, i, 0)),
        out_shape=jax.ShapeDtypeStruct((nb, sq, qw), BF16),
        scratch_shapes=[pltpu.VMEM((n_kv, n_lat + n_ctx, HEAD_DIM), BF16),
                        pltpu.VMEM((kw, n_lat + n_ctx), BF16),
                        pltpu.VMEM((2, n_lat + n_ctx, per_task * tq), F32),
                        pltpu.VMEM((2, n_lat + n_ctx, per_task * tq), BF16)],
        compiler_params=_cparams("arbitrary", "arbitrary"),
        name="gqa_lat" if n_lat else "gqa_ctx",
    )(*args)


def _diff_kernel(*refs, n_lat, n_ctx, tq, rope, lam_init):
    refs = list(refs)
    h_ref = refs.pop(0) if n_lat else None
    hc_ref = refs.pop(0)
    wqt_ref, wk_ref, wvt_ref, wg_ref, gqt_ref, gk_ref, bd_ref, lam_ref, sgt_ref = refs[:9]
    refs = refs[9:]
    cosk_ref = sink_ref = cost_ref = sint_ref = None
    if rope:
        cosk_ref, sink_ref, cost_ref, sint_ref = refs[:4]
        refs = refs[4:]
    o_ref, k_scr, vt_scr = refs
    q_src = h_ref if n_lat else hc_ref
    i = pl.program_id(1)

    @pl.when(i == 0)
    def _():
        if n_lat:
            _stage_k(h_ref, n_lat, 0, wk_ref, gk_ref[...], bd_ref[...], k_scr, (cosk_ref, sink_ref) if rope else None)
            _stage_vt(h_ref, n_lat, 0, wvt_ref, vt_scr)
        _stage_k(hc_ref, n_ctx, n_lat, wk_ref, gk_ref[...], bd_ref[...], k_scr, None)
        _stage_vt(hc_ref, n_ctx, n_lat, wvt_ref, vt_scr)

    lam_p = lam_ref[...]
    lam = (jnp.exp(jnp.sum(lam_p[0:1] * lam_p[1:2], axis=-1, keepdims=True))
           - jnp.exp(jnp.sum(lam_p[2:3] * lam_p[3:4], axis=-1, keepdims=True)) + lam_init)

    r0 = pl.multiple_of(i * tq, tq)
    hh = q_src[0, pl.ds(r0, tq), :]
    qh = _q_heads_t(hh, wqt_ref, gqt_ref, (cost_ref[...], sint_ref[...]) if rope else None)
    gate = _dot(hh, wg_ref[...])
    sub_gain = _lane_tile(sgt_ref[...], tq) * (1.0 - lam_init)

    tasks = []
    for sub in range(2 * H_D):
        v_rows = slice((sub // 2) * 2 * HEAD_DIM, (sub // 2 + 1) * 2 * HEAD_DIM)
        tasks.append((lambda rows, sub=sub: k_scr[sub, rows, :], qh[sub],
                      lambda rows, v_rows=v_rows: vt_scr[v_rows, rows]))
    res = _attend_whole(tasks, n_lat + n_ctx)
    blocks = []
    for hd in range(H_D):
        (pv0, l0), (pv1, l1) = res[2 * hd], res[2 * hd + 1]
        o = pv0 / l0 - lam * (pv1 / l1)
        blocks.append(o * lax.rsqrt(jnp.mean(o * o, axis=0, keepdims=True) + EPS) * sub_gain)
    yt = jnp.concatenate(blocks, axis=0)
    o_ref[0] = (yt.T * _silu(gate)).astype(BF16)


def _diff_attn(h, hc, wq, wk, wv, wg, gq, gk, bd, lam_p, subln_g, rope, lam_init, tq):
    args, specs, n_lat, n_ctx = _attn_args(h, hc, wq, wk, wv, wg, gq, gk, bd)
    nb = hc.shape[0]
    sq = n_lat if n_lat else n_ctx
    tq = min(tq, sq)
    sgt = jnp.broadcast_to(subln_g[:, None], (2 * HEAD_DIM, LANES))
    args += [lam_p, sgt]
    specs += [_const_spec(lam_p.shape), _const_spec(sgt.shape)]
    if rope is not None:
        a, s = _rope_args(rope, tq)
        args += a
        specs += s
    return pl.pallas_call(
        functools.partial(_diff_kernel, n_lat=n_lat, n_ctx=n_ctx, tq=tq, rope=rope is not None, lam_init=lam_init),
        grid=(nb, sq // tq),
        in_specs=specs,
        out_specs=pl.BlockSpec((1, tq, W_D), lambda b, i: (b, i, 0)),
        out_shape=jax.ShapeDtypeStruct((nb, sq, W_D), BF16),
        scratch_shapes=[pltpu.VMEM((2 * H_D, n_lat + n_ctx, HEAD_DIM), BF16),
                        pltpu.VMEM((W_D, n_lat + n_ctx), BF16)],
        compiler_params=_cparams("arbitrary", "arbitrary"),
        name="diff_lat" if n_lat else "diff_ctx",
    )(*args)


NA_TILE = 4
NA_BAND = NA_TILE + NA_ROWS
NA_VROWS = NA_ROWS // 2
NA_VBLK = NA_VROWS * GRID_W


def _rpb_table_kernel(rpb_ref, onehot_ref, mask_ref, o_ref):
    r = rpb_ref[...]
    hi = r.astype(BF16)
    mid = (r - hi.astype(F32)).astype(BF16)
    lo = (r - hi.astype(F32) - mid.astype(F32)).astype(BF16)
    oh = onehot_ref[...]
    o_ref[...] = (_dot(hi, oh) + _dot(mid, oh) + _dot(lo, oh) + mask_ref[...]) * LOG2E


def _rpb_tables(na_rpb):
    n_layer = na_rpb.shape[0]
    n_dr, n_dc = 2 * NA_ROWS - 1, 2 * NA_COLS - 1
    cq = np.arange(GRID_W)
    c0 = np.clip(cq - NA_COLS // 2, 0, GRID_W - NA_COLS)
    col_ok = (cq[None, :] >= c0[:, None]) & (cq[None, :] < c0[:, None] + NA_COLS)
    dc = np.clip(cq[None, :] - cq[:, None] + (NA_COLS - 1), 0, n_dc - 1)
    onehot = (np.arange(32)[:, None, None] == dc.T[None]) & col_ok.T[None]
    onehot = jnp.asarray(onehot.reshape(32, GRID_W * GRID_W), BF16)
    mask = jnp.asarray(np.where(col_ok.T, 0.0, NEG_INF).reshape(1, GRID_W * GRID_W), F32)
    rows = n_layer * H_B * n_dr
    rows_pad = -(-rows // 8) * 8
    rpb2 = jnp.zeros((rows_pad, 32), F32).at[:rows, :n_dc].set(na_rpb.reshape(rows, n_dc))
    tab = pl.pallas_call(
        _rpb_table_kernel,
        out_shape=jax.ShapeDtypeStruct((rows_pad, GRID_W * GRID_W), F32),
        compiler_params=pltpu.CompilerParams(vmem_limit_bytes=VMEM_LIMIT),
        name="rpb_table",
    )(rpb2, onehot, mask)
    tab = tab[:rows].reshape(n_layer, H_B, n_dr, GRID_W, GRID_W)
    zero = jnp.zeros_like(tab[:, :, :1])
    padded = jnp.concatenate([zero, tab, zero], axis=2)
    return jnp.concatenate([padded[:, :, 1:], padded[:, :, :-1]], axis=-1)


def _natten_kernel(h_ref, hc_ref, wqt_ref, wk_ref, wvt_ref, wg_ref, gqt_ref, gk_ref, bd_ref, tab_ref,
                   o_ref, k_scr, vt_scr, vtc_scr, *, n_lat, n_ctx):
    t = pl.program_id(1)
    n_rows = n_lat // GRID_W
    tq = NA_TILE * GRID_W
    band = NA_BAND * GRID_W

    @pl.when(t == 0)
    def _():
        _stage_k(h_ref, n_lat, 0, wk_ref, gk_ref[...], bd_ref[...], k_scr, None)
        _stage_k(hc_ref, n_ctx, n_lat, wk_ref, gk_ref[...], bd_ref[...], k_scr, None)
        for c in range(n_lat // NA_VBLK):
            vt_scr[c] = _dot_nt(wvt_ref[...], h_ref[0, c * NA_VBLK:(c + 1) * NA_VBLK, :]).astype(BF16)
        _stage_vt(hc_ref, n_ctx, 0, wvt_ref, vtc_scr)

    blk0 = jnp.clip(t * (NA_TILE // NA_VROWS) - 1, 0, n_rows // NA_VROWS - NA_BAND // NA_VROWS)
    u0 = blk0 * NA_VROWS
    k0 = pl.multiple_of(u0 * GRID_W, NA_VBLK)

    lane_row = lax.broadcasted_iota(jnp.int32, (1, tq), 1) >> GRID_SHIFT
    mask_rows, bias_idx = [], []
    for u in range(NA_BAND):
        row = jnp.zeros((1, tq), F32)
        for i in range(NA_TILE):
            r_band = jnp.clip(t * NA_TILE + i - NA_ROWS // 2, 0, n_rows - NA_ROWS)
            ok = (u0 + u >= r_band) & (u0 + u < r_band + NA_ROWS)
            row = jnp.where(lane_row == i, jnp.where(ok, 0.0, NEG_INF), row)
        mask_rows.append(row)
        j = u0 + u - t * NA_TILE + (NA_ROWS - 1)
        bias_idx.append([jnp.clip(j - 2 * ip, 0, 2 * NA_ROWS - 1) for ip in range(NA_TILE // 2)])

    q0 = pl.multiple_of(t * tq, tq)
    hh = h_ref[0, pl.ds(q0, tq), :]
    qh = _q_heads_t(hh, wqt_ref, gqt_ref, None)
    gate = _dot(hh, wg_ref[...])

    def scores(hd):
        return (_dot(k_scr[hd, pl.ds(k0, band), :], qh[hd]),
                _dot(k_scr[hd, n_lat:n_lat + n_ctx, :], qh[hd]))

    blocks = []
    ahead = scores(0)
    for hd in range(H_B):
        st, sc = ahead
        if hd + 1 < H_B:
            ahead = scores(hd + 1)
        rows = []
        for u in range(NA_BAND):
            bias = jnp.concatenate([tab_ref[hd, jj] for jj in bias_idx[u]], axis=1)
            rows.append(st[u * GRID_W:(u + 1) * GRID_W, :] + bias + mask_rows[u])
        st = jnp.concatenate(rows, axis=0)
        m = jnp.maximum(jnp.max(st, axis=0, keepdims=True), jnp.max(sc, axis=0, keepdims=True))
        p = jnp.exp2(st - m)
        pc = jnp.exp2(sc - m)
        l = jnp.sum(p, axis=0, keepdims=True) + jnp.sum(pc, axis=0, keepdims=True)
        pb = p.astype(BF16)
        hs = slice(hd * HEAD_DIM, (hd + 1) * HEAD_DIM)
        ot = _dot(vtc_scr[hs, :], pc.astype(BF16))
        for c in range(NA_BAND // NA_VROWS):
            ot = ot + _dot(vt_scr[blk0 + c, hs, :], pb[c * NA_VBLK:(c + 1) * NA_VBLK, :])
        blocks.append(ot / l)
    yt = jnp.concatenate(blocks, axis=0)
    o_ref[0] = (yt.T * _silu(gate)).astype(BF16)


def _natten(h, hc, wq, wk, wv, wg, gq, gk, bd, tab):
    nb, n_lat, d = h.shape
    n_ctx = hc.shape[1]
    n_rows = n_lat // GRID_W
    assert n_rows % NA_TILE == 0 and n_rows >= NA_BAND
    tq = NA_TILE * GRID_W
    full = lambda b, t: (b, 0, 0)
    gqt = jnp.broadcast_to(gq[:, None], (HEAD_DIM, LANES))
    consts = [wq.T, wk, wv.T, wg, gqt, jnp.tile(gk, H_B).reshape(1, W_B), bd, tab]
    return pl.pallas_call(
        functools.partial(_natten_kernel, n_lat=n_lat, n_ctx=n_ctx),
        grid=(nb, n_lat // tq),
        in_specs=[pl.BlockSpec((1, n_lat, d), full), pl.BlockSpec((1, n_ctx, d), full)]
                 + [_const_spec(a.shape) for a in consts],
        out_specs=pl.BlockSpec((1, tq, W_B), lambda b, t: (b, t, 0)),
        out_shape=jax.ShapeDtypeStruct((nb, n_lat, W_B), BF16),
        scratch_shapes=[pltpu.VMEM((H_B, n_lat + n_ctx, HEAD_DIM), BF16),
                        pltpu.VMEM((n_lat // NA_VBLK, W_B, NA_VBLK), BF16),
                        pltpu.VMEM((W_B, n_ctx), BF16)],
        compiler_params=_cparams("arbitrary", "arbitrary"),
        name="natten",
    )(h, hc, *consts)


def _merge_kernel(h_ref, ya_ref, yb_ref, yc_ref, yd_ref, wl_ref, bm_ref, wbr_ref, wo_ref, x_ref, mod_ref, o_ref):
    d = x_ref.shape[2]
    hh = h_ref[0]
    m = None
    for br, y_ref in enumerate((ya_ref, yb_ref, yc_ref, yd_ref)):
        cols = slice(br * d, (br + 1) * d)
        g = _sigmoid(_dot(hh, wl_ref[:, cols]) + bm_ref[:, cols])
        t = g * _dot(y_ref[0], wbr_ref[br])
        m = t if m is None else m + t
    gate = mod_ref[0, :, 2 * d:3 * d]
    o_ref[0] = x_ref[0] + gate * _dot(m.astype(BF16), wo_ref[...])


def _merge(h, ys, wl, b_merge, wbr, wo, x, mod, per_batch):
    nb, seq, d = x.shape
    tm = min(seq, 512)
    tile = lambda b, i: (b, i, 0)
    mod_map = (lambda b, i: (b, 0, 0)) if per_batch else (lambda b, i: (0, 0, 0))
    return pl.pallas_call(
        _merge_kernel,
        grid=(nb, seq // tm),
        in_specs=[pl.BlockSpec((1, tm, d), tile)]
                 + [pl.BlockSpec((1, tm, W_A), tile)] * N_BRANCH
                 + [_const_spec(wl.shape), _const_spec((1, N_BRANCH * d)), _const_spec(wbr.shape), _const_spec(wo.shape),
                    pl.BlockSpec((1, tm, d), tile),
                    pl.BlockSpec((1, 1, 3 * d), mod_map)],
        out_specs=pl.BlockSpec((1, tm, d), tile),
        out_shape=jax.ShapeDtypeStruct((nb, seq, d), F32),
        compiler_params=_cparams("arbitrary", "arbitrary"),
        name="merge",
    )(h, *ys, wl, b_merge.reshape(1, N_BRANCH * d), wbr, wo, x, mod)


def _rope_tables(seq):
    half = HEAD_DIM // 2
    t = np.arange(seq)
    freqs = ROPE_THETA ** (-np.arange(0, half, 2, dtype=np.float64) / half)
    blocks_c, blocks_s = [], []
    for pos in (t // GRID_W, t % GRID_W):
        ang = pos[:, None].astype(np.float64) * freqs[None, :]
        blocks_c += [np.cos(ang), np.cos(ang)]
        blocks_s += [-np.sin(ang), np.sin(ang)]
    cos = np.concatenate(blocks_c, axis=1)
    sin = np.concatenate(blocks_s, axis=1)
    reps = LANES // HEAD_DIM
    key_side = (jnp.asarray(np.tile(cos, (1, reps)), F32), jnp.asarray(np.tile(sin, (1, reps)), F32))
    query_side = (jnp.asarray(cos.T, F32), jnp.asarray(sin.T, F32))
    return key_side, query_side


def _group_mean_matrix():
    idx = np.arange(LANES) // HEAD_DIM
    return jnp.asarray((idx[:, None] == idx[None, :]) / HEAD_DIM, BF16)


def _qkvg(w, off, qw, kw, vw, gw):
    edges = np.cumsum([off, qw, kw, vw, gw])
    return tuple(w[:, a:b] for a, b in zip(edges[:-1], edges[1:]))


def kernel(x, c, ctx, c_ctx, w_ada, b_ada, norm_g, w_in, b_merge, conv_w, conv_b, conv_ln_g, conv_ln_b, na_qn_g, na_kn_g, na_rpb, gqa_qn_g, gqa_kn_g, diff_qn_g, diff_kn_g, lam_q1, lam_k1, lam_q2, lam_k2, diff_subln_g, w_br_a, w_br_b, w_br_c, w_br_d, w_out):
    n_batch, seq, d = x.shape
    depth = w_in.shape[0]
    rope = _rope_tables(seq)
    bd = _group_mean_matrix()
    tabs = _rpb_tables(na_rpb)

    rows = -(-(n_batch + 1) // 8) * 8
    cc = jnp.zeros((rows, d), F32).at[:n_batch].set(c).at[n_batch].set(c_ctx)
    mod_all = _modulation(cc, w_ada, b_ada)

    xc = ctx
    for l in range(depth):
        need_ctx = l < depth - 1
        lam_init = 0.8 - 0.6 * math.exp(-0.3 * l)
        mod_x = mod_all[l, :n_batch, None, :]
        mod_c = mod_all[l, n_batch:n_batch + 1, None, :]
        h = _hnorm(x, mod_x, norm_g[l], True)
        hc = _hnorm(xc, mod_c, norm_g[l], False)
        w = w_in[l].astype(BF16)
        w_a = w[:, OFF_A:OFF_B]
        w_b = _qkvg(w, OFF_B, W_B, W_B, W_B, W_B)
        w_c = _qkvg(w, OFF_C, W_C, W_C_KV, W_C_KV, W_C)
        w_d = _qkvg(w, OFF_D, W_D, W_D, W_D, W_D)
        w_l = w[:, OFF_L:]
        wbr = jnp.stack([w_br_a[l], w_br_b[l], w_br_c[l], w_br_d[l]]).astype(BF16)
        wo = w_out[l].astype(BF16)
        lam_p = jnp.stack([lam_q1[l], lam_k1[l], lam_q2[l], lam_k2[l]])

        ya = _conv_mixer(h, w_a, conv_w[l], conv_b[l], conv_ln_g[l], conv_ln_b[l])
        yb = _natten(h, hc, *w_b, na_qn_g[l], na_kn_g[l], bd, tabs[l])
        yc = _gqa(h, hc, *w_c, gqa_qn_g[l], gqa_kn_g[l], bd, rope, H_C, KV_C, 256)
        yd = _diff_attn(h, hc, *w_d, diff_qn_g[l], diff_kn_g[l], bd, lam_p, diff_subln_g[l], rope, lam_init, 512)
        x_new = _merge(h, (ya, yb, yc, yd), w_l, b_merge[l], wbr, wo, x, mod_x, True)

        if need_ctx:
            ca = _conv_mixer(hc, w_a, conv_w[l], conv_b[l], conv_ln_g[l], conv_ln_b[l])
            cb = _gqa(None, hc, *w_b, na_qn_g[l], na_kn_g[l], bd, None, H_B, H_B, 256)
            cg = _gqa(None, hc, *w_c, gqa_qn_g[l], gqa_kn_g[l], bd, None, H_C, KV_C, 256)
            cd = _diff_attn(None, hc, *w_d, diff_qn_g[l], diff_kn_g[l], bd, lam_p, diff_subln_g[l], None, lam_init, 256)
            xc = _merge(hc, (ca, cb, cg, cd), w_l, b_merge[l], wbr, wo, xc, mod_c, False)
        x = x_new
    return x
```

```python
import functools
import math

import numpy as np
import jax
import jax.numpy as jnp
from jax import lax
from jax.experimental import pallas as pl
from jax.experimental.pallas import tpu as pltpu

F32 = jnp.float32
BF16 = jnp.bfloat16

GRID_W = 64
GRID_SHIFT = GRID_W.bit_length() - 1
HEAD_DIM = 64
N_BRANCH = 4
W_A = 512
CONV_K = 31
CONV_PAD = 16
H_B = 8
W_B = H_B * HEAD_DIM
NA_ROWS = 8
NA_COLS = 16
H_C = 8
KV_C = 2
W_C = H_C * HEAD_DIM
W_C_KV = KV_C * HEAD_DIM
H_D = 4
W_D = H_D * 2 * HEAD_DIM
ROPE_THETA = 10000.0
EPS = 1e-6
NEG_INF = -1e30
LOG2E = math.log2(math.e)
SM_SCALE = HEAD_DIM ** -0.5

OFF_A = 0
OFF_B = OFF_A + 3 * W_A
OFF_C = OFF_B + 4 * W_B
OFF_D = OFF_C + 2 * W_C + 2 * W_C_KV
OFF_L = OFF_D + 4 * W_D
LANES = 128
ATTN_LANES = 512
ATTN_KEY_TILE = 256
VMEM_LIMIT = 56 * 1024 * 1024


def _cparams(*sem):
    return pltpu.CompilerParams(dimension_semantics=sem, vmem_limit_bytes=VMEM_LIMIT)


def _const_spec(shape):
    nd = len(shape)
    return pl.BlockSpec(shape, lambda *_: (0,) * nd, pipeline_mode=pl.Buffered(1))


def _sigmoid(x):
    return 1.0 / (1.0 + jnp.exp(-x))


def _silu(x):
    return x * _sigmoid(x)


def _dot(a, b):
    return jnp.dot(a, b, preferred_element_type=F32)


def _dot_nt(a, b):
    return lax.dot_general(a, b, (((1,), (1,)), ((), ())), preferred_element_type=F32)


def _split_bf16(x):
    hi = x.astype(BF16)
    lo = (x - hi.astype(F32)).astype(BF16)
    return hi, lo


def _head_rms(x, bd):
    x2 = x * x
    hi, lo = _split_bf16(x2)
    cols = []
    for c in range(x.shape[1] // LANES):
        sl = slice(c * LANES, (c + 1) * LANES)
        cols.append(_dot(hi[:, sl], bd) + _dot(lo[:, sl], bd))
    ms = cols[0] if len(cols) == 1 else jnp.concatenate(cols, axis=1)
    return x * lax.rsqrt(ms + EPS)


def _lane_tile(t, width):
    reps = width // t.shape[1]
    return t if reps == 1 else jnp.concatenate([t] * reps, axis=1)


def _rope(x, cos, sin):
    n = x.shape[1]
    fwd = pltpu.roll(x, n - 16, 1)
    bwd = pltpu.roll(x, 16, 1)
    lane = lax.broadcasted_iota(jnp.int32, x.shape, 1)
    swapped = jnp.where((lane & 16) == 0, fwd, bwd)
    return x * _lane_tile(cos, n) + swapped * _lane_tile(sin, n)


def _mod_kernel(cc_ref, w_ref, b_ref, o_ref):
    a = _silu(cc_ref[...])
    a_hi, a_lo = _split_bf16(a)
    w_hi, w_lo = _split_bf16(w_ref[0])
    o_ref[0] = _dot(a_hi, w_hi) + _dot(a_hi, w_lo) + _dot(a_lo, w_hi) + b_ref[0]


def _modulation(cc, w_ada, b_ada):
    n_layer, d, d3 = w_ada.shape
    rows = cc.shape[0]
    tn = 768
    return pl.pallas_call(
        _mod_kernel,
        grid=(n_layer, d3 // tn),
        in_specs=[pl.BlockSpec((rows, d), lambda l, n: (0, 0)),
                  pl.BlockSpec((1, d, tn), lambda l, n: (l, 0, n)),
                  pl.BlockSpec((1, 1, tn), lambda l, n: (l, 0, n))],
        out_specs=pl.BlockSpec((1, rows, tn), lambda l, n: (l, 0, n)),
        out_shape=jax.ShapeDtypeStruct((n_layer, rows, d3), F32),
        compiler_params=_cparams("arbitrary", "arbitrary"),
        name="adaln_mod",
    )(cc, w_ada, b_ada.reshape(n_layer, 1, d3))


def _hnorm_kernel(x_ref, mod_ref, g_ref, h_ref):
    d = x_ref.shape[2]
    x = x_ref[0]
    y = x * lax.rsqrt(jnp.mean(x * x, axis=-1, keepdims=True) + EPS) * g_ref[...]
    shift = mod_ref[0, :, 0:d]
    scale = mod_ref[0, :, d:2 * d]
    h_ref[0] = (y * (1.0 + scale) + shift).astype(BF16)


def _hnorm(x, mod, g, per_batch):
    nb, seq, d = x.shape
    tm = min(seq, 1024)
    mod_map = (lambda b, i: (b, 0, 0)) if per_batch else (lambda b, i: (0, 0, 0))
    return pl.pallas_call(
        _hnorm_kernel,
        grid=(nb, seq // tm),
        in_specs=[pl.BlockSpec((1, tm, d), lambda b, i: (b, i, 0)),
                  pl.BlockSpec((1, 1, 3 * d), mod_map),
                  pl.BlockSpec((1, d), lambda b, i: (0, 0))],
        out_specs=pl.BlockSpec((1, tm, d), lambda b, i: (b, i, 0)),
        out_shape=jax.ShapeDtypeStruct((nb, seq, d), BF16),
        compiler_params=_cparams("arbitrary", "arbitrary"),
        name="mod_rmsnorm",
    )(x, mod, g.reshape(1, d))


def _conv_kernel(h_ref, w_ref, cw_ref, cb_ref, lg_ref, lb_ref, o_ref, hcv_ref, *, seq, chunk):
    zeros = jnp.zeros((CONV_PAD, W_A), F32)
    hcv_ref[0:CONV_PAD, :] = zeros
    hcv_ref[seq + CONV_PAD:seq + 2 * CONV_PAD, :] = zeros

    glu_rows = min(seq, 2 * chunk)

    def glu(c, carry):
        r0 = pl.multiple_of(c * glu_rows, glu_rows)
        ag = _dot(h_ref[0, pl.ds(r0, glu_rows), :], w_ref[:, 0:2 * W_A])
        hcv_ref[pl.ds(r0 + CONV_PAD, glu_rows), :] = ag[:, :W_A] * _sigmoid(ag[:, W_A:])
        return carry

    lax.fori_loop(0, seq // glu_rows, glu, 0)

    def conv(c, carry):
        r0 = pl.multiple_of(c * chunk, chunk)
        acc = jnp.zeros((chunk, W_A), F32) + cb_ref[...]
        win = hcv_ref[pl.ds(r0, chunk + 2 * CONV_PAD), :]
        n_win = chunk + 2 * CONV_PAD
        for b in range(8):
            offs = [j + CONV_PAD - CONV_K // 2 for j in range(CONV_K) if (j + CONV_PAD - CONV_K // 2) % 8 == b]
            shifted = win if b == 0 else pltpu.roll(win, n_win - b, 0)
            for off in offs:
                j = off - CONV_PAD + CONV_K // 2
                acc = acc + cw_ref[j:j + 1, :] * shifted[off - b:off - b + chunk, :]
        mu = jnp.mean(acc, axis=-1, keepdims=True)
        cen = acc - mu
        var = jnp.mean(cen * cen, axis=-1, keepdims=True)
        y = _silu(cen * lax.rsqrt(var + EPS) * lg_ref[...] + lb_ref[...])
        gate = _dot(h_ref[0, pl.ds(r0, chunk), :], w_ref[:, 2 * W_A:3 * W_A])
        o_ref[0, pl.ds(r0, chunk), :] = (y * _silu(gate)).astype(BF16)
        return carry

    lax.fori_loop(0, seq // chunk, conv, 0)


def _conv_mixer(h, w, conv_w, conv_b, ln_g, ln_b):
    nb, seq, d = h.shape
    chunk = min(seq, 128)
    cw = jnp.concatenate([conv_w, jnp.zeros((32 - CONV_K, W_A), F32)], axis=0)
    return pl.pallas_call(
        functools.partial(_conv_kernel, seq=seq, chunk=chunk),
        grid=(nb,),
        in_specs=[pl.BlockSpec((1, seq, d), lambda b: (b, 0, 0)),
                  _const_spec((d, 3 * W_A)),
                  _const_spec((32, W_A)),
                  _const_spec((1, W_A)), _const_spec((1, W_A)), _const_spec((1, W_A))],
        out_specs=pl.BlockSpec((1, seq, W_A), lambda b: (b, 0, 0)),
        out_shape=jax.ShapeDtypeStruct((nb, seq, W_A), BF16),
        scratch_shapes=[pltpu.VMEM((seq + 2 * CONV_PAD, W_A), F32)],
        compiler_params=_cparams("arbitrary"),
        name="conv_mixer",
    )(h, w, cw, conv_b.reshape(1, W_A), ln_g.reshape(1, W_A), ln_b.reshape(1, W_A))


def _stage_k(src_ref, n_rows, row0, wk_ref, gk, bd, k_scr, rope_tabs):
    kc = min(n_rows, 512)
    for c in range(n_rows // kc):
        rs = slice(c * kc, (c + 1) * kc)
        k = _head_rms(_dot(src_ref[0, rs, :], wk_ref[...]), bd) * gk
        if rope_tabs is not None:
            k = _rope(k, rope_tabs[0][rs, :], rope_tabs[1][rs, :])
        dst = slice(row0 + c * kc, row0 + (c + 1) * kc)
        for n in range(k.shape[1] // HEAD_DIM):
            k_scr[n, dst, :] = k[:, n * HEAD_DIM:(n + 1) * HEAD_DIM].astype(BF16)


def _stage_vt(src_ref, n_rows, row0, wvt_ref, vt_scr):
    kc = min(n_rows, 512)
    for c in range(n_rows // kc):
        vt = _dot_nt(wvt_ref[...], src_ref[0, c * kc:(c + 1) * kc, :])
        vt_scr[:, row0 + c * kc:row0 + (c + 1) * kc] = vt.astype(BF16)


def _q_heads_t(hh, wqt_ref, gqt_ref, rope_t):
    qt = _dot_nt(wqt_ref[...], hh)
    tq = qt.shape[1]
    x = qt.reshape(qt.shape[0] // HEAD_DIM, HEAD_DIM, tq)
    x = x * lax.rsqrt(jnp.mean(x * x, axis=1, keepdims=True) + EPS)
    x = x * (_lane_tile(gqt_ref[...], tq) * (SM_SCALE * LOG2E))[None]
    if rope_t is not None:
        swapped = jnp.concatenate([x[:, 16:32], x[:, 0:16], x[:, 48:64], x[:, 32:48]], axis=1)
        x = x * rope_t[0][None] + swapped * rope_t[1][None]
    return x.astype(BF16)


def _attend(tasks, s_scr, p_scr, n_keys):
    kt = ATTN_KEY_TILE if n_keys % ATTN_KEY_TILE == 0 else LANES
    w = tasks[0][1].shape[1]
    results = []
    m_prev = None
    for step in range(len(tasks) + 1):
        scoring = step < len(tasks)
        reducing = step >= 1
        if scoring:
            k_fn, q_blk, _ = tasks[step]
            m_run = jnp.full((8, w), NEG_INF, F32)
        if reducing:
            v_fn = tasks[step - 1][2]
            l_run = jnp.zeros((8, w), F32)
        for t in range(n_keys // kt):
            rows = slice(t * kt, (t + 1) * kt)
            if scoring:
                s = _dot(k_fn(rows), q_blk)
                s_scr[step % 2, rows, :] = s
                m_run = jnp.maximum(m_run, jnp.max(s.reshape(kt // 8, 8, w), axis=0))
            if reducing:
                p = jnp.exp2(s_scr[(step - 1) % 2, rows, :] - m_prev)
                l_run = l_run + jnp.sum(p.reshape(kt // 8, 8, w), axis=0)
                p_scr[(step - 1) % 2, rows, :] = p.astype(BF16)
        if reducing:
            acc = _dot(v_fn(slice(0, n_keys)), p_scr[(step - 1) % 2])
            results.append((acc, jnp.sum(l_run, axis=0, keepdims=True)))
        if scoring:
            m_prev = jnp.max(m_run, axis=0, keepdims=True)
    return results


def _attn_args(h, hc, wq, wk, wv, wg, gq, gk, bd):
    nb, n_ctx, d = hc.shape
    n_lat = 0 if h is None else h.shape[1]
    full = lambda b, i: (b, 0, 0)
    args, specs = [], []
    if n_lat:
        args.append(h)
        specs.append(pl.BlockSpec((1, n_lat, d), full))
    args.append(hc)
    specs.append(pl.BlockSpec((1, n_ctx, d), full))
    kw = wk.shape[1]
    gqt = jnp.broadcast_to(gq[:, None], (HEAD_DIM, LANES))
    consts = [wq.T, wk, wv.T, wg, gqt, jnp.tile(gk, kw // HEAD_DIM).reshape(1, kw), bd]
    return args + consts, specs + [_const_spec(a.shape) for a in consts], n_lat, n_ctx


def _rope_args(rope, tq):
    (cos_k, sin_k), (cos_t, sin_t) = rope
    tile = pl.BlockSpec((HEAD_DIM, tq), lambda b, i: (0, i))
    return [cos_k, sin_k, cos_t, sin_t], [_const_spec(cos_k.shape), _const_spec(sin_k.shape), tile, tile]


def _gqa_kernel(*refs, n_lat, n_ctx, n_q, n_kv, tq, per_task, rope):
    refs = list(refs)
    h_ref = refs.pop(0) if n_lat else None
    hc_ref = refs.pop(0)
    wqt_ref, wk_ref, wvt_ref, wg_ref, gqt_ref, gk_ref, bd_ref = refs[:7]
    refs = refs[7:]
    cosk_ref = sink_ref = cost_ref = sint_ref = None
    if rope:
        cosk_ref, sink_ref, cost_ref, sint_ref = refs[:4]
        refs = refs[4:]
    o_ref, k_scr, vt_scr, s_scr, p_scr = refs
    q_src = h_ref if n_lat else hc_ref
    group = n_q // n_kv
    i = pl.program_id(1)

    @pl.when(i == 0)
    def _():
        if n_lat:
            _stage_k(h_ref, n_lat, 0, wk_ref, gk_ref[...], bd_ref[...], k_scr, (cosk_ref, sink_ref) if rope else None)
            _stage_vt(h_ref, n_lat, 0, wvt_ref, vt_scr)
        _stage_k(hc_ref, n_ctx, n_lat, wk_ref, gk_ref[...], bd_ref[...], k_scr, None)
        _stage_vt(hc_ref, n_ctx, n_lat, wvt_ref, vt_scr)

    r0 = pl.multiple_of(i * tq, tq)
    hh = q_src[0, pl.ds(r0, tq), :]
    qh = _q_heads_t(hh, wqt_ref, gqt_ref, (cost_ref[...], sint_ref[...]) if rope else None)
    gate = _dot(hh, wg_ref[...])

    tasks = []
    for n in range(n_kv):
        for first in range(0, group, per_task):
            heads = [qh[n * group + first + g] for g in range(per_task)]
            q_blk = heads[0] if per_task == 1 else jnp.concatenate(heads, axis=1)
            tasks.append((lambda rows, n=n: k_scr[n, rows, :], q_blk,
                          lambda rows, n=n: vt_scr[n * HEAD_DIM:(n + 1) * HEAD_DIM, rows]))
    blocks = []
    for acc, l in _attend(tasks, s_scr, p_scr, n_lat + n_ctx):
        ot = acc / l
        blocks.extend(ot[:, g * tq:(g + 1) * tq] for g in range(per_task))
    yt = jnp.concatenate(blocks, axis=0)
    o_ref[0] = (yt.T * _silu(gate)).astype(BF16)


def _gqa(h, hc, wq, wk, wv, wg, gq, gk, bd, rope, n_q, n_kv, tq):
    args, specs, n_lat, n_ctx = _attn_args(h, hc, wq, wk, wv, wg, gq, gk, bd)
    nb = hc.shape[0]
    sq = n_lat if n_lat else n_ctx
    tq = min(tq, sq)
    qw, kw = n_q * HEAD_DIM, n_kv * HEAD_DIM
    per_task = max(1, min(n_q // n_kv, ATTN_LANES // tq))
    if rope is not None:
        a, s = _rope_args(rope, tq)
        args += a
        specs += s
    return pl.pallas_call(
        functools.partial(_gqa_kernel, n_lat=n_lat, n_ctx=n_ctx, n_q=n_q, n_kv=n_kv, tq=tq, per_task=per_task,
                          rope=rope is not None),
        grid=(nb, sq // tq),
        in_specs=specs,
        out_specs=pl.BlockSpec((1, tq, qw), lambda b, i: (b, i, 0)),
        out_shape=jax.ShapeDtypeStruct((nb, sq, qw), BF16),
        scratch_shapes=[pltpu.VMEM((n_kv, n_lat + n_ctx, HEAD_DIM), BF16),
                        pltpu.VMEM((kw, n_lat + n_ctx), BF16),
                        pltpu.VMEM((2, n_lat + n_ctx, per_task * tq), F32),
                        pltpu.VMEM((2, n_lat + n_ctx, per_task * tq), BF16)],
        compiler_params=_cparams("arbitrary", "arbitrary"),
        name="gqa_lat" if n_lat else "gqa_ctx",
    )(*args)


def _diff_kernel(*refs, n_lat, n_ctx, tq, rope, lam_init):
    refs = list(refs)
    h_ref = refs.pop(0) if n_lat else None
    hc_ref = refs.pop(0)
    wqt_ref, wk_ref, wvt_ref, wg_ref, gqt_ref, gk_ref, bd_ref, lam_ref, sgt_ref = refs[:9]
    refs = refs[9:]
    cosk_ref = sink_ref = cost_ref = sint_ref = None
    if rope:
        cosk_ref, sink_ref, cost_ref, sint_ref = refs[:4]
        refs = refs[4:]
    o_ref, k_scr, vt_scr = refs
    q_src = h_ref if n_lat else hc_ref
    i = pl.program_id(1)

    @pl.when(i == 0)
    def _():
        if n_lat:
            _stage_k(h_ref, n_lat, 0, wk_ref, gk_ref[...], bd_ref[...], k_scr, (cosk_ref, sink_ref) if rope else None)
            _stage_vt(h_ref, n_lat, 0, wvt_ref, vt_scr)
        _stage_k(hc_ref, n_ctx, n_lat, wk_ref, gk_ref[...], bd_ref[...], k_scr, None)
        _stage_vt(hc_ref, n_ctx, n_lat, wvt_ref, vt_scr)

    lam_p = lam_ref[...]
    lam = (jnp.exp(jnp.sum(lam_p[0:1] * lam_p[1:2], axis=-1, keepdims=True))
           - jnp.exp(jnp.sum(lam_p[2:3] * lam_p[3:4], axis=-1, keepdims=True)) + lam_init)

    r0 = pl.multiple_of(i * tq, tq)
    hh = q_src[0, pl.ds(r0, tq), :]
    qh = _q_heads_t(hh, wqt_ref, gqt_ref, (cost_ref[...], sint_ref[...]) if rope else None)
    gate = _dot(hh, wg_ref[...])
    sub_gain = _lane_tile(sgt_ref[...], tq) * (1.0 - lam_init)

    blocks = []
    for hd in range(H_D):
        es, ls = [], []
        for t in range(2):
            s = _dot(k_scr[2 * hd + t], qh[2 * hd + t])
            p = jnp.exp2(s - jnp.max(s, axis=0, keepdims=True))
            es.append(p.astype(BF16))
            ls.append(jnp.sum(p, axis=0, keepdims=True))
        pv = _dot(vt_scr[hd * 2 * HEAD_DIM:(hd + 1) * 2 * HEAD_DIM, :], jnp.concatenate(es, axis=1))
        o = pv[:, :tq] / ls[0] - lam * (pv[:, tq:] / ls[1])
        blocks.append(o * lax.rsqrt(jnp.mean(o * o, axis=0, keepdims=True) + EPS) * sub_gain)
    yt = jnp.concatenate(blocks, axis=0)
    o_ref[0] = (yt.T * _silu(gate)).astype(BF16)


def _diff_attn(h, hc, wq, wk, wv, wg, gq, gk, bd, lam_p, subln_g, rope, lam_init, tq):
    args, specs, n_lat, n_ctx = _attn_args(h, hc, wq, wk, wv, wg, gq, gk, bd)
    nb = hc.shape[0]
    sq = n_lat if n_lat else n_ctx
    tq = min(tq, sq)
    sgt = jnp.broadcast_to(subln_g[:, None], (2 * HEAD_DIM, LANES))
    args += [lam_p, sgt]
    specs += [_const_spec(lam_p.shape), _const_spec(sgt.shape)]
    if rope is not None:
        a, s = _rope_args(rope, tq)
        args += a
        specs += s
    return pl.pallas_call(
        functools.partial(_diff_kernel, n_lat=n_lat, n_ctx=n_ctx, tq=tq, rope=rope is not None, lam_init=lam_init),
        grid=(nb, sq // tq),
        in_specs=specs,
        out_specs=pl.BlockSpec((1, tq, W_D), lambda b, i: (b, i, 0)),
        out_shape=jax.ShapeDtypeStruct((nb, sq, W_D), BF16),
        scratch_shapes=[pltpu.VMEM((2 * H_D, n_lat + n_ctx, HEAD_DIM), BF16),
                        pltpu.VMEM((W_D, n_lat + n_ctx), BF16)],
        compiler_params=_cparams("arbitrary", "arbitrary"),
        name="diff_lat" if n_lat else "diff_ctx",
    )(*args)


NA_TILE = 4
NA_BAND = NA_TILE + NA_ROWS
NA_VROWS = NA_ROWS // 2
NA_VBLK = NA_VROWS * GRID_W


def _rpb_table_kernel(rpb_ref, onehot_ref, mask_ref, o_ref):
    r = rpb_ref[...]
    hi = r.astype(BF16)
    mid = (r - hi.astype(F32)).astype(BF16)
    lo = (r - hi.astype(F32) - mid.astype(F32)).astype(BF16)
    oh = onehot_ref[...]
    o_ref[...] = (_dot(hi, oh) + _dot(mid, oh) + _dot(lo, oh) + mask_ref[...]) * LOG2E


def _rpb_tables(na_rpb):
    n_layer = na_rpb.shape[0]
    n_dr, n_dc = 2 * NA_ROWS - 1, 2 * NA_COLS - 1
    cq = np.arange(GRID_W)
    c0 = np.clip(cq - NA_COLS // 2, 0, GRID_W - NA_COLS)
    col_ok = (cq[None, :] >= c0[:, None]) & (cq[None, :] < c0[:, None] + NA_COLS)
    dc = np.clip(cq[None, :] - cq[:, None] + (NA_COLS - 1), 0, n_dc - 1)
    onehot = (np.arange(32)[:, None, None] == dc.T[None]) & col_ok.T[None]
    onehot = jnp.asarray(onehot.reshape(32, GRID_W * GRID_W), BF16)
    mask = jnp.asarray(np.where(col_ok.T, 0.0, NEG_INF).reshape(1, GRID_W * GRID_W), F32)
    rows = n_layer * H_B * n_dr
    rows_pad = -(-rows // 8) * 8
    rpb2 = jnp.zeros((rows_pad, 32), F32).at[:rows, :n_dc].set(na_rpb.reshape(rows, n_dc))
    tab = pl.pallas_call(
        _rpb_table_kernel,
        out_shape=jax.ShapeDtypeStruct((rows_pad, GRID_W * GRID_W), F32),
        compiler_params=pltpu.CompilerParams(vmem_limit_bytes=VMEM_LIMIT),
        name="rpb_table",
    )(rpb2, onehot, mask)
    tab = tab[:rows].reshape(n_layer, H_B, n_dr, GRID_W, GRID_W)
    zero = jnp.zeros_like(tab[:, :, :1])
    padded = jnp.concatenate([zero, tab, zero], axis=2)
    return jnp.concatenate([padded[:, :, 1:], padded[:, :, :-1]], axis=-1)


def _natten_kernel(h_ref, hc_ref, wqt_ref, wk_ref, wvt_ref, wg_ref, gqt_ref, gk_ref, bd_ref, tab_ref,
                   o_ref, k_scr, vt_scr, vtc_scr, *, n_lat, n_ctx):
    t = pl.program_id(1)
    n_rows = n_lat // GRID_W
    tq = NA_TILE * GRID_W
    band = NA_BAND * GRID_W

    @pl.when(t == 0)
    def _():
        _stage_k(h_ref, n_lat, 0, wk_ref, gk_ref[...], bd_ref[...], k_scr, None)
        _stage_k(hc_ref, n_ctx, n_lat, wk_ref, gk_ref[...], bd_ref[...], k_scr, None)
        for c in range(n_lat // NA_VBLK):
            vt_scr[c] = _dot_nt(wvt_ref[...], h_ref[0, c * NA_VBLK:(c + 1) * NA_VBLK, :]).astype(BF16)
        _stage_vt(hc_ref, n_ctx, 0, wvt_ref, vtc_scr)

    blk0 = jnp.clip(t * (NA_TILE // NA_VROWS) - 1, 0, n_rows // NA_VROWS - NA_BAND // NA_VROWS)
    u0 = blk0 * NA_VROWS
    k0 = pl.multiple_of(u0 * GRID_W, NA_VBLK)

    lane_row = lax.broadcasted_iota(jnp.int32, (1, tq), 1) >> GRID_SHIFT
    mask_rows, bias_idx = [], []
    for u in range(NA_BAND):
        row = jnp.zeros((1, tq), F32)
        for i in range(NA_TILE):
            r_band = jnp.clip(t * NA_TILE + i - NA_ROWS // 2, 0, n_rows - NA_ROWS)
            ok = (u0 + u >= r_band) & (u0 + u < r_band + NA_ROWS)
            row = jnp.where(lane_row == i, jnp.where(ok, 0.0, NEG_INF), row)
        mask_rows.append(row)
        j = u0 + u - t * NA_TILE + (NA_ROWS - 1)
        bias_idx.append([jnp.clip(j - 2 * ip, 0, 2 * NA_ROWS - 1) for ip in range(NA_TILE // 2)])

    q0 = pl.multiple_of(t * tq, tq)
    hh = h_ref[0, pl.ds(q0, tq), :]
    qh = _q_heads_t(hh, wqt_ref, gqt_ref, None)
    gate = _dot(hh, wg_ref[...])

    def scores(hd):
        return (_dot(k_scr[hd, pl.ds(k0, band), :], qh[hd]),
                _dot(k_scr[hd, n_lat:n_lat + n_ctx, :], qh[hd]))

    blocks = []
    ahead = scores(0)
    for hd in range(H_B):
        st, sc = ahead
        if hd + 1 < H_B:
            ahead = scores(hd + 1)
        rows = []
        for u in range(NA_BAND):
            bias = jnp.concatenate([tab_ref[hd, jj] for jj in bias_idx[u]], axis=1)
            rows.append(st[u * GRID_W:(u + 1) * GRID_W, :] + bias + mask_rows[u])
        st = jnp.concatenate(rows, axis=0)
        m = jnp.maximum(jnp.max(st, axis=0, keepdims=True), jnp.max(sc, axis=0, keepdims=True))
        p = jnp.exp2(st - m)
        pc = jnp.exp2(sc - m)
        l = jnp.sum(p, axis=0, keepdims=True) + jnp.sum(pc, axis=0, keepdims=True)
        pb = p.astype(BF16)
        hs = slice(hd * HEAD_DIM, (hd + 1) * HEAD_DIM)
        ot = _dot(vtc_scr[hs, :], pc.astype(BF16))
        for c in range(NA_BAND // NA_VROWS):
            ot = ot + _dot(vt_scr[blk0 + c, hs, :], pb[c * NA_VBLK:(c + 1) * NA_VBLK, :])
        blocks.append(ot / l)
    yt = jnp.concatenate(blocks, axis=0)
    o_ref[0] = (yt.T * _silu(gate)).astype(BF16)


def _natten(h, hc, wq, wk, wv, wg, gq, gk, bd, tab):
    nb, n_lat, d = h.shape
    n_ctx = hc.shape[1]
    n_rows = n_lat // GRID_W
    assert n_rows % NA_TILE == 0 and n_rows >= NA_BAND
    tq = NA_TILE * GRID_W
    full = lambda b, t: (b, 0, 0)
    gqt = jnp.broadcast_to(gq[:, None], (HEAD_DIM, LANES))
    consts = [wq.T, wk, wv.T, wg, gqt, jnp.tile(gk, H_B).reshape(1, W_B), bd, tab]
    return pl.pallas_call(
        functools.partial(_natten_kernel, n_lat=n_lat, n_ctx=n_ctx),
        grid=(nb, n_lat // tq),
        in_specs=[pl.BlockSpec((1, n_lat, d), full), pl.BlockSpec((1, n_ctx, d), full)]
                 + [_const_spec(a.shape) for a in consts],
        out_specs=pl.BlockSpec((1, tq, W_B), lambda b, t: (b, t, 0)),
        out_shape=jax.ShapeDtypeStruct((nb, n_lat, W_B), BF16),
        scratch_shapes=[pltpu.VMEM((H_B, n_lat + n_ctx, HEAD_DIM), BF16),
                        pltpu.VMEM((n_lat // NA_VBLK, W_B, NA_VBLK), BF16),
                        pltpu.VMEM((W_B, n_ctx), BF16)],
        compiler_params=_cparams("arbitrary", "arbitrary"),
        name="natten",
    )(h, hc, *consts)


def _merge_kernel(h_ref, ya_ref, yb_ref, yc_ref, yd_ref, wl_ref, bm_ref, wbr_ref, wo_ref, x_ref, mod_ref, o_ref):
    d = x_ref.shape[2]
    hh = h_ref[0]
    m = None
    for br, y_ref in enumerate((ya_ref, yb_ref, yc_ref, yd_ref)):
        cols = slice(br * d, (br + 1) * d)
        g = _sigmoid(_dot(hh, wl_ref[:, cols]) + bm_ref[:, cols])
        t = g * _dot(y_ref[0], wbr_ref[br])
        m = t if m is None else m + t
    gate = mod_ref[0, :, 2 * d:3 * d]
    o_ref[0] = x_ref[0] + gate * _dot(m.astype(BF16), wo_ref[...])


def _merge(h, ys, wl, b_merge, wbr, wo, x, mod, per_batch):
    nb, seq, d = x.shape
    tm = min(seq, 512)
    tile = lambda b, i: (b, i, 0)
    mod_map = (lambda b, i: (b, 0, 0)) if per_batch else (lambda b, i: (0, 0, 0))
    return pl.pallas_call(
        _merge_kernel,
        grid=(nb, seq // tm),
        in_specs=[pl.BlockSpec((1, tm, d), tile)]
                 + [pl.BlockSpec((1, tm, W_A), tile)] * N_BRANCH
                 + [_const_spec(wl.shape), _const_spec((1, N_BRANCH * d)), _const_spec(wbr.shape), _const_spec(wo.shape),
                    pl.BlockSpec((1, tm, d), tile),
                    pl.BlockSpec((1, 1, 3 * d), mod_map)],
        out_specs=pl.BlockSpec((1, tm, d), tile),
        out_shape=jax.ShapeDtypeStruct((nb, seq, d), F32),
        compiler_params=_cparams("arbitrary", "arbitrary"),
        name="merge",
    )(h, *ys, wl, b_merge.reshape(1, N_BRANCH * d), wbr, wo, x, mod)


def _rope_tables(seq):
    half = HEAD_DIM // 2
    t = np.arange(seq)
    freqs = ROPE_THETA ** (-np.arange(0, half, 2, dtype=np.float64) / half)
    blocks_c, blocks_s = [], []
    for pos in (t // GRID_W, t % GRID_W):
        ang = pos[:, None].astype(np.float64) * freqs[None, :]
        blocks_c += [np.cos(ang), np.cos(ang)]
        blocks_s += [-np.sin(ang), np.sin(ang)]
    cos = np.concatenate(blocks_c, axis=1)
    sin = np.concatenate(blocks_s, axis=1)
    reps = LANES // HEAD_DIM
    key_side = (jnp.asarray(np.tile(cos, (1, reps)), F32), jnp.asarray(np.tile(sin, (1, reps)), F32))
    query_side = (jnp.asarray(cos.T, F32), jnp.asarray(sin.T, F32))
    return key_side, query_side


def _group_mean_matrix():
    idx = np.arange(LANES) // HEAD_DIM
    return jnp.asarray((idx[:, None] == idx[None, :]) / HEAD_DIM, BF16)


def _qkvg(w, off, qw, kw, vw, gw):
    edges = np.cumsum([off, qw, kw, vw, gw])
    return tuple(w[:, a:b] for a, b in zip(edges[:-1], edges[1:]))


def kernel(x, c, ctx, c_ctx, w_ada, b_ada, norm_g, w_in, b_merge, conv_w, conv_b, conv_ln_g, conv_ln_b, na_qn_g, na_kn_g, na_rpb, gqa_qn_g, gqa_kn_g, diff_qn_g, diff_kn_g, lam_q1, lam_k1, lam_q2, lam_k2, diff_subln_g, w_br_a, w_br_b, w_br_c, w_br_d, w_out):
    n_batch, seq, d = x.shape
    depth = w_in.shape[0]
    rope = _rope_tables(seq)
    bd = _group_mean_matrix()
    tabs = _rpb_tables(na_rpb)

    rows = -(-(n_batch + 1) // 8) * 8
    cc = jnp.zeros((rows, d), F32).at[:n_batch].set(c).at[n_batch].set(c_ctx)
    mod_all = _modulation(cc, w_ada, b_ada)

    xc = ctx
    for l in range(depth):
        need_ctx = l < depth - 1
        lam_init = 0.8 - 0.6 * math.exp(-0.3 * l)
        mod_x = mod_all[l, :n_batch, None, :]
        mod_c = mod_all[l, n_batch:n_batch + 1, None, :]
        h = _hnorm(x, mod_x, norm_g[l], True)
        hc = _hnorm(xc, mod_c, norm_g[l], False)
        w = w_in[l].astype(BF16)
        w_a = w[:, OFF_A:OFF_B]
        w_b = _qkvg(w, OFF_B, W_B, W_B, W_B, W_B)
        w_c = _qkvg(w, OFF_C, W_C, W_C_KV, W_C_KV, W_C)
        w_d = _qkvg(w, OFF_D, W_D, W_D, W_D, W_D)
        w_l = w[:, OFF_L:]
        wbr = jnp.stack([w_br_a[l], w_br_b[l], w_br_c[l], w_br_d[l]]).astype(BF16)
        wo = w_out[l].astype(BF16)
        lam_p = jnp.stack([lam_q1[l], lam_k1[l], lam_q2[l], lam_k2[l]])

        ya = _conv_mixer(h, w_a, conv_w[l], conv_b[l], conv_ln_g[l], conv_ln_b[l])
        yb = _natten(h, hc, *w_b, na_qn_g[l], na_kn_g[l], bd, tabs[l])
        yc = _gqa(h, hc, *w_c, gqa_qn_g[l], gqa_kn_g[l], bd, rope, H_C, KV_C, 256)
        yd = _diff_attn(h, hc, *w_d, diff_qn_g[l], diff_kn_g[l], bd, lam_p, diff_subln_g[l], rope, lam_init, 512)
        x_new = _merge(h, (ya, yb, yc, yd), w_l, b_merge[l], wbr, wo, x, mod_x, True)

        if need_ctx:
            ca = _conv_mixer(hc, w_a, conv_w[l], conv_b[l], conv_ln_g[l], conv_ln_b[l])
            cb = _gqa(None, hc, *w_b, na_qn_g[l], na_kn_g[l], bd, None, H_B, H_B, 256)
            cg = _gqa(None, hc, *w_c, gqa_qn_g[l], gqa_kn_g[l], bd, None, H_C, KV_C, 256)
            cd = _diff_attn(None, hc, *w_d, diff_qn_g[l], diff_kn_g[l], bd, lam_p, diff_subln_g[l], None, lam_init, 256)
            xc = _merge(hc, (ca, cb, cg, cd), w_l, b_merge[l], wbr, wo, xc, mod_c, False)
        x = x_new
    return x
```

```python
import functools
import math

import numpy as np
import jax
import jax.numpy as jnp
from jax import lax
from jax.experimental import pallas as pl
from jax.experimental.pallas import tpu as pltpu

F32 = jnp.float32
BF16 = jnp.bfloat16

GRID_W = 64
GRID_SHIFT = GRID_W.bit_length() - 1
HEAD_DIM = 64
N_BRANCH = 4
W_A = 512
CONV_K = 31
CONV_PAD = 16
H_B = 8
W_B = H_B * HEAD_DIM
NA_ROWS = 8
NA_COLS = 16
H_C = 8
KV_C = 2
W_C = H_C * HEAD_DIM
W_C_KV = KV_C * HEAD_DIM
H_D = 4
W_D = H_D * 2 * HEAD_DIM
ROPE_THETA = 10000.0
EPS = 1e-6
NEG_INF = -1e30
LOG2E = math.log2(math.e)
SM_SCALE = HEAD_DIM ** -0.5

OFF_A = 0
OFF_B = OFF_A + 3 * W_A
OFF_C = OFF_B + 4 * W_B
OFF_D = OFF_C + 2 * W_C + 2 * W_C_KV
OFF_L = OFF_D + 4 * W_D
LANES = 128
ATTN_LANES = 512
ATTN_KEY_TILE = 256
VMEM_LIMIT = 56 * 1024 * 1024


def _cparams(*sem):
    return pltpu.CompilerParams(dimension_semantics=sem, vmem_limit_bytes=VMEM_LIMIT)


def _const_spec(shape):
    nd = len(shape)
    return pl.BlockSpec(shape, lambda *_: (0,) * nd, pipeline_mode=pl.Buffered(1))


def _sigmoid(x):
    return 1.0 / (1.0 + jnp.exp(-x))


def _silu(x):
    return x * _sigmoid(x)


def _dot(a, b):
    return jnp.dot(a, b, preferred_element_type=F32)


def _dot_nt(a, b):
    return lax.dot_general(a, b, (((1,), (1,)), ((), ())), preferred_element_type=F32)


def _split_bf16(x):
    hi = x.astype(BF16)
    lo = (x - hi.astype(F32)).astype(BF16)
    return hi, lo


def _head_rms(x, bd):
    x2 = x * x
    hi, lo = _split_bf16(x2)
    cols = []
    for c in range(x.shape[1] // LANES):
        sl = slice(c * LANES, (c + 1) * LANES)
        cols.append(_dot(hi[:, sl], bd) + _dot(lo[:, sl], bd))
    ms = cols[0] if len(cols) == 1 else jnp.concatenate(cols, axis=1)
    return x * lax.rsqrt(ms + EPS)


def _lane_tile(t, width):
    reps = width // t.shape[1]
    return t if reps == 1 else jnp.concatenate([t] * reps, axis=1)


def _rope(x, cos, sin):
    n = x.shape[1]
    fwd = pltpu.roll(x, n - 16, 1)
    bwd = pltpu.roll(x, 16, 1)
    lane = lax.broadcasted_iota(jnp.int32, x.shape, 1)
    swapped = jnp.where((lane & 16) == 0, fwd, bwd)
    return x * _lane_tile(cos, n) + swapped * _lane_tile(sin, n)


def _mod_kernel(cc_ref, w_ref, b_ref, o_ref):
    a = _silu(cc_ref[...])
    a_hi, a_lo = _split_bf16(a)
    w_hi, w_lo = _split_bf16(w_ref[0])
    o_ref[0] = _dot(a_hi, w_hi) + _dot(a_hi, w_lo) + _dot(a_lo, w_hi) + b_ref[0]


def _modulation(cc, w_ada, b_ada):
    n_layer, d, d3 = w_ada.shape
    rows = cc.shape[0]
    tn = 768
    return pl.pallas_call(
        _mod_kernel,
        grid=(n_layer, d3 // tn),
        in_specs=[pl.BlockSpec((rows, d), lambda l, n: (0, 0)),
                  pl.BlockSpec((1, d, tn), lambda l, n: (l, 0, n)),
                  pl.BlockSpec((1, 1, tn), lambda l, n: (l, 0, n))],
        out_specs=pl.BlockSpec((1, rows, tn), lambda l, n: (l, 0, n)),
        out_shape=jax.ShapeDtypeStruct((n_layer, rows, d3), F32),
        compiler_params=_cparams("arbitrary", "arbitrary"),
        name="adaln_mod",
    )(cc, w_ada, b_ada.reshape(n_layer, 1, d3))


def _hnorm_kernel(x_ref, mod_ref, g_ref, h_ref):
    d = x_ref.shape[2]
    x = x_ref[0]
    y = x * lax.rsqrt(jnp.mean(x * x, axis=-1, keepdims=True) + EPS) * g_ref[...]
    shift = mod_ref[0, :, 0:d]
    scale = mod_ref[0, :, d:2 * d]
    h_ref[0] = (y * (1.0 + scale) + shift).astype(BF16)


def _hnorm(x, mod, g, per_batch):
    nb, seq, d = x.shape
    tm = min(seq, 1024)
    mod_map = (lambda b, i: (b, 0, 0)) if per_batch else (lambda b, i: (0, 0, 0))
    return pl.pallas_call(
        _hnorm_kernel,
        grid=(nb, seq // tm),
        in_specs=[pl.BlockSpec((1, tm, d), lambda b, i: (b, i, 0)),
                  pl.BlockSpec((1, 1, 3 * d), mod_map),
                  pl.BlockSpec((1, d), lambda b, i: (0, 0))],
        out_specs=pl.BlockSpec((1, tm, d), lambda b, i: (b, i, 0)),
        out_shape=jax.ShapeDtypeStruct((nb, seq, d), BF16),
        compiler_params=_cparams("arbitrary", "arbitrary"),
        name="mod_rmsnorm",
    )(x, mod, g.reshape(1, d))


def _conv_kernel(h_ref, w_ref, cw_ref, cb_ref, lg_ref, lb_ref, o_ref, hcv_ref, *, seq, chunk):
    zeros = jnp.zeros((CONV_PAD, W_A), F32)
    hcv_ref[0:CONV_PAD, :] = zeros
    hcv_ref[seq + CONV_PAD:seq + 2 * CONV_PAD, :] = zeros

    glu_rows = min(seq, 2 * chunk)

    def glu(c, carry):
        r0 = pl.multiple_of(c * glu_rows, glu_rows)
        ag = _dot(h_ref[0, pl.ds(r0, glu_rows), :], w_ref[:, 0:2 * W_A])
        hcv_ref[pl.ds(r0 + CONV_PAD, glu_rows), :] = ag[:, :W_A] * _sigmoid(ag[:, W_A:])
        return carry

    lax.fori_loop(0, seq // glu_rows, glu, 0)

    def conv(c, carry):
        r0 = pl.multiple_of(c * chunk, chunk)
        n_win = chunk + 2 * CONV_PAD
        halves = []
        for cols in (slice(0, W_A // 2), slice(W_A // 2, W_A)):
            acc = jnp.zeros((chunk, W_A // 2), F32) + cb_ref[:, cols]
            win = hcv_ref[pl.ds(r0, n_win), cols]
            for b in range(8):
                offs = [j + CONV_PAD - CONV_K // 2 for j in range(CONV_K) if (j + CONV_PAD - CONV_K // 2) % 8 == b]
                shifted = win if b == 0 else pltpu.roll(win, n_win - b, 0)
                for off in offs:
                    j = off - CONV_PAD + CONV_K // 2
                    acc = acc + cw_ref[j:j + 1, cols] * shifted[off - b:off - b + chunk, :]
            halves.append(acc)
        acc = jnp.concatenate(halves, axis=1)
        mu = jnp.mean(acc, axis=-1, keepdims=True)
        cen = acc - mu
        var = jnp.mean(cen * cen, axis=-1, keepdims=True)
        y = _silu(cen * lax.rsqrt(var + EPS) * lg_ref[...] + lb_ref[...])
        gate = _dot(h_ref[0, pl.ds(r0, chunk), :], w_ref[:, 2 * W_A:3 * W_A])
        o_ref[0, pl.ds(r0, chunk), :] = (y * _silu(gate)).astype(BF16)
        return carry

    lax.fori_loop(0, seq // chunk, conv, 0)


def _conv_mixer(h, w, conv_w, conv_b, ln_g, ln_b):
    nb, seq, d = h.shape
    chunk = min(seq, 128)
    cw = jnp.concatenate([conv_w, jnp.zeros((32 - CONV_K, W_A), F32)], axis=0)
    return pl.pallas_call(
        functools.partial(_conv_kernel, seq=seq, chunk=chunk),
        grid=(nb,),
        in_specs=[pl.BlockSpec((1, seq, d), lambda b: (b, 0, 0)),
                  _const_spec((d, 3 * W_A)),
                  _const_spec((32, W_A)),
                  _const_spec((1, W_A)), _const_spec((1, W_A)), _const_spec((1, W_A))],
        out_specs=pl.BlockSpec((1, seq, W_A), lambda b: (b, 0, 0)),
        out_shape=jax.ShapeDtypeStruct((nb, seq, W_A), BF16),
        scratch_shapes=[pltpu.VMEM((seq + 2 * CONV_PAD, W_A), F32)],
        compiler_params=_cparams("arbitrary"),
        name="conv_mixer",
    )(h, w, cw, conv_b.reshape(1, W_A), ln_g.reshape(1, W_A), ln_b.reshape(1, W_A))


def _stage_k(src_ref, n_rows, row0, wk_ref, gk, bd, k_scr, rope_tabs):
    kc = min(n_rows, 512)
    for c in range(n_rows // kc):
        rs = slice(c * kc, (c + 1) * kc)
        k = _head_rms(_dot(src_ref[0, rs, :], wk_ref[...]), bd) * gk
        if rope_tabs is not None:
            k = _rope(k, rope_tabs[0][rs, :], rope_tabs[1][rs, :])
        dst = slice(row0 + c * kc, row0 + (c + 1) * kc)
        for n in range(k.shape[1] // HEAD_DIM):
            k_scr[n, dst, :] = k[:, n * HEAD_DIM:(n + 1) * HEAD_DIM].astype(BF16)


def _stage_vt(src_ref, n_rows, row0, wvt_ref, vt_scr):
    kc = min(n_rows, 512)
    for c in range(n_rows // kc):
        vt = _dot_nt(wvt_ref[...], src_ref[0, c * kc:(c + 1) * kc, :])
        vt_scr[:, row0 + c * kc:row0 + (c + 1) * kc] = vt.astype(BF16)


def _q_heads_t(hh, wqt_ref, gqt_ref, rope_t):
    qt = _dot_nt(wqt_ref[...], hh)
    tq = qt.shape[1]
    x = qt.reshape(qt.shape[0] // HEAD_DIM, HEAD_DIM, tq)
    x = x * lax.rsqrt(jnp.mean(x * x, axis=1, keepdims=True) + EPS)
    x = x * (_lane_tile(gqt_ref[...], tq) * (SM_SCALE * LOG2E))[None]
    if rope_t is not None:
        swapped = jnp.concatenate([x[:, 16:32], x[:, 0:16], x[:, 48:64], x[:, 32:48]], axis=1)
        x = x * rope_t[0][None] + swapped * rope_t[1][None]
    return x.astype(BF16)


def _attend(tasks, s_scr, p_scr, n_keys):
    kt = ATTN_KEY_TILE if n_keys % ATTN_KEY_TILE == 0 else LANES
    w = tasks[0][1].shape[1]
    results = []
    m_prev = None
    for step in range(len(tasks) + 1):
        scoring = step < len(tasks)
        reducing = step >= 1
        if scoring:
            k_fn, q_blk, _ = tasks[step]
            m_run = jnp.full((8, w), NEG_INF, F32)
        if reducing:
            v_fn = tasks[step - 1][2]
            l_run = jnp.zeros((8, w), F32)
        for t in range(n_keys // kt):
            rows = slice(t * kt, (t + 1) * kt)
            if scoring:
                s = _dot(k_fn(rows), q_blk)
                s_scr[step % 2, rows, :] = s
                m_run = jnp.maximum(m_run, jnp.max(s.reshape(kt // 8, 8, w), axis=0))
            if reducing:
                p = jnp.exp2(s_scr[(step - 1) % 2, rows, :] - m_prev)
                l_run = l_run + jnp.sum(p.reshape(kt // 8, 8, w), axis=0)
                p_scr[(step - 1) % 2, rows, :] = p.astype(BF16)
        if reducing:
            acc = _dot(v_fn(slice(0, n_keys)), p_scr[(step - 1) % 2])
            results.append((acc, jnp.sum(l_run, axis=0, keepdims=True)))
        if scoring:
            m_prev = jnp.max(m_run, axis=0, keepdims=True)
    return results


def _attn_args(h, hc, wq, wk, wv, wg, gq, gk, bd):
    nb, n_ctx, d = hc.shape
    n_lat = 0 if h is None else h.shape[1]
    full = lambda b, i: (b, 0, 0)
    args, specs = [], []
    if n_lat:
        args.append(h)
        specs.append(pl.BlockSpec((1, n_lat, d), full))
    args.append(hc)
    specs.append(pl.BlockSpec((1, n_ctx, d), full))
    kw = wk.shape[1]
    gqt = jnp.broadcast_to(gq[:, None], (HEAD_DIM, LANES))
    consts = [wq.T, wk, wv.T, wg, gqt, jnp.tile(gk, kw // HEAD_DIM).reshape(1, kw), bd]
    return args + consts, specs + [_const_spec(a.shape) for a in consts], n_lat, n_ctx


def _rope_args(rope, tq):
    (cos_k, sin_k), (cos_t, sin_t) = rope
    tile = pl.BlockSpec((HEAD_DIM, tq), lambda b, i: (0, i))
    return [cos_k, sin_k, cos_t, sin_t], [_const_spec(cos_k.shape), _const_spec(sin_k.shape), tile, tile]


def _gqa_kernel(*refs, n_lat, n_ctx, n_q, n_kv, tq, per_task, rope):
    refs = list(refs)
    h_ref = refs.pop(0) if n_lat else None
    hc_ref = refs.pop(0)
    wqt_ref, wk_ref, wvt_ref, wg_ref, gqt_ref, gk_ref, bd_ref = refs[:7]
    refs = refs[7:]
    cosk_ref = sink_ref = cost_ref = sint_ref = None
    if rope:
        cosk_ref, sink_ref, cost_ref, sint_ref = refs[:4]
        refs = refs[4:]
    o_ref, k_scr, vt_scr, s_scr, p_scr = refs
    q_src = h_ref if n_lat else hc_ref
    group = n_q // n_kv
    i = pl.program_id(1)

    @pl.when(i == 0)
    def _():
        if n_lat:
            _stage_k(h_ref, n_lat, 0, wk_ref, gk_ref[...], bd_ref[...], k_scr, (cosk_ref, sink_ref) if rope else None)
            _stage_vt(h_ref, n_lat, 0, wvt_ref, vt_scr)
        _stage_k(hc_ref, n_ctx, n_lat, wk_ref, gk_ref[...], bd_ref[...], k_scr, None)
        _stage_vt(hc_ref, n_ctx, n_lat, wvt_ref, vt_scr)

    r0 = pl.multiple_of(i * tq, tq)
    hh = q_src[0, pl.ds(r0, tq), :]
    qh = _q_heads_t(hh, wqt_ref, gqt_ref, (cost_ref[...], sint_ref[...]) if rope else None)
    gate = _dot(hh, wg_ref[...])

    tasks = []
    for n in range(n_kv):
        for first in range(0, group, per_task):
            heads = [qh[n * group + first + g] for g in range(per_task)]
            q_blk = heads[0] if per_task == 1 else jnp.concatenate(heads, axis=1)
            tasks.append((lambda rows, n=n: k_scr[n, rows, :], q_blk,
                          lambda rows, n=n: vt_scr[n * HEAD_DIM:(n + 1) * HEAD_DIM, rows]))
    blocks = []
    for acc, l in _attend(tasks, s_scr, p_scr, n_lat + n_ctx):
        ot = acc / l
        blocks.extend(ot[:, g * tq:(g + 1) * tq] for g in range(per_task))
    yt = jnp.concatenate(blocks, axis=0)
    o_ref[0] = (yt.T * _silu(gate)).astype(BF16)


def _gqa(h, hc, wq, wk, wv, wg, gq, gk, bd, rope, n_q, n_kv, tq):
    args, specs, n_lat, n_ctx = _attn_args(h, hc, wq, wk, wv, wg, gq, gk, bd)
    nb = hc.shape[0]
    sq = n_lat if n_lat else n_ctx
    tq = min(tq, sq)
    qw, kw = n_q * HEAD_DIM, n_kv * HEAD_DIM
    per_task = max(1, min(n_q // n_kv, ATTN_LANES // tq))
    if rope is not None:
        a, s = _rope_args(rope, tq)
        args += a
        specs += s
    return pl.pallas_call(
        functools.partial(_gqa_kernel, n_lat=n_lat, n_ctx=n_ctx, n_q=n_q, n_kv=n_kv, tq=tq, per_task=per_task,
                          rope=rope is not None),
        grid=(nb, sq // tq),
        in_specs=specs,
        out_specs=pl.BlockSpec((1, tq, qw), lambda b, i: (b, i, 0)),
        out_shape=jax.ShapeDtypeStruct((nb, sq, qw), BF16),
        scratch_shapes=[pltpu.VMEM((n_kv, n_lat + n_ctx, HEAD_DIM), BF16),
                        pltpu.VMEM((kw, n_lat + n_ctx), BF16),
                        pltpu.VMEM((2, n_lat + n_ctx, per_task * tq), F32),
                        pltpu.VMEM((2, n_lat + n_ctx, per_task * tq), BF16)],
        compiler_params=_cparams("arbitrary", "arbitrary"),
        name="gqa_lat" if n_lat else "gqa_ctx",
    )(*args)


def _diff_kernel(*refs, n_lat, n_ctx, tq, rope, lam_init):
    refs = list(refs)
    h_ref = refs.pop(0) if n_lat else None
    hc_ref = refs.pop(0)
    wqt_ref, wk_ref, wvt_ref, wg_ref, gqt_ref, gk_ref, bd_ref, lam_ref, sgt_ref = refs[:9]
    refs = refs[9:]
    cosk_ref = sink_ref = cost_ref = sint_ref = None
    if rope:
        cosk_ref, sink_ref, cost_ref, sint_ref = refs[:4]
        refs = refs[4:]
    o_ref, k_scr, vt_scr = refs
    q_src = h_ref if n_lat else hc_ref
    i = pl.program_id(1)

    @pl.when(i == 0)
    def _():
        if n_lat:
            _stage_k(h_ref, n_lat, 0, wk_ref, gk_ref[...], bd_ref[...], k_scr, (cosk_ref, sink_ref) if rope else None)
            _stage_vt(h_ref, n_lat, 0, wvt_ref, vt_scr)
        _stage_k(hc_ref, n_ctx, n_lat, wk_ref, gk_ref[...], bd_ref[...], k_scr, None)
        _stage_vt(hc_ref, n_ctx, n_lat, wvt_ref, vt_scr)

    lam_p = lam_ref[...]
    lam = (jnp.exp(jnp.sum(lam_p[0:1] * lam_p[1:2], axis=-1, keepdims=True))
           - jnp.exp(jnp.sum(lam_p[2:3] * lam_p[3:4], axis=-1, keepdims=True)) + lam_init)

    r0 = pl.multiple_of(i * tq, tq)
    hh = q_src[0, pl.ds(r0, tq), :]
    qh = _q_heads_t(hh, wqt_ref, gqt_ref, (cost_ref[...], sint_ref[...]) if rope else None)
    gate = _dot(hh, wg_ref[...])
    sub_gain = _lane_tile(sgt_ref[...], tq) * (1.0 - lam_init)

    blocks = []
    for hd in range(H_D):
        es, ls = [], []
        for t in range(2):
            s = _dot(k_scr[2 * hd + t], qh[2 * hd + t])
            p = jnp.exp2(s - jnp.max(s, axis=0, keepdims=True))
            es.append(p.astype(BF16))
            ls.append(jnp.sum(p, axis=0, keepdims=True))
        pv = _dot(vt_scr[hd * 2 * HEAD_DIM:(hd + 1) * 2 * HEAD_DIM, :], jnp.concatenate(es, axis=1))
        o = pv[:, :tq] / ls[0] - lam * (pv[:, tq:] / ls[1])
        blocks.append(o * lax.rsqrt(jnp.mean(o * o, axis=0, keepdims=True) + EPS) * sub_gain)
    yt = jnp.concatenate(blocks, axis=0)
    o_ref[0] = (yt.T * _silu(gate)).astype(BF16)


def _diff_attn(h, hc, wq, wk, wv, wg, gq, gk, bd, lam_p, subln_g, rope, lam_init, tq):
    args, specs, n_lat, n_ctx = _attn_args(h, hc, wq, wk, wv, wg, gq, gk, bd)
    nb = hc.shape[0]
    sq = n_lat if n_lat else n_ctx
    tq = min(tq, sq)
    sgt = jnp.broadcast_to(subln_g[:, None], (2 * HEAD_DIM, LANES))
    args += [lam_p, sgt]
    specs += [_const_spec(lam_p.shape), _const_spec(sgt.shape)]
    if rope is not None:
        a, s = _rope_args(rope, tq)
        args += a
        specs += s
    return pl.pallas_call(
        functools.partial(_diff_kernel, n_lat=n_lat, n_ctx=n_ctx, tq=tq, rope=rope is not None, lam_init=lam_init),
        grid=(nb, sq // tq),
        in_specs=specs,
        out_specs=pl.BlockSpec((1, tq, W_D), lambda b, i: (b, i, 0)),
        out_shape=jax.ShapeDtypeStruct((nb, sq, W_D), BF16),
        scratch_shapes=[pltpu.VMEM((2 * H_D, n_lat + n_ctx, HEAD_DIM), BF16),
                        pltpu.VMEM((W_D, n_lat + n_ctx), BF16)],
        compiler_params=_cparams("arbitrary", "arbitrary"),
        name="diff_lat" if n_lat else "diff_ctx",
    )(*args)


NA_TILE = 4
NA_BAND = NA_TILE + NA_ROWS
NA_VROWS = NA_ROWS // 2
NA_VBLK = NA_VROWS * GRID_W


def _rpb_table_kernel(rpb_ref, onehot_ref, mask_ref, o_ref):
    r = rpb_ref[...]
    hi = r.astype(BF16)
    mid = (r - hi.astype(F32)).astype(BF16)
    lo = (r - hi.astype(F32) - mid.astype(F32)).astype(BF16)
    oh = onehot_ref[...]
    o_ref[...] = (_dot(hi, oh) + _dot(mid, oh) + _dot(lo, oh) + mask_ref[...]) * LOG2E


def _rpb_tables(na_rpb):
    n_layer = na_rpb.shape[0]
    n_dr, n_dc = 2 * NA_ROWS - 1, 2 * NA_COLS - 1
    cq = np.arange(GRID_W)
    c0 = np.clip(cq - NA_COLS // 2, 0, GRID_W - NA_COLS)
    col_ok = (cq[None, :] >= c0[:, None]) & (cq[None, :] < c0[:, None] + NA_COLS)
    dc = np.clip(cq[None, :] - cq[:, None] + (NA_COLS - 1), 0, n_dc - 1)
    onehot = (np.arange(32)[:, None, None] == dc.T[None]) & col_ok.T[None]
    onehot = jnp.asarray(onehot.reshape(32, GRID_W * GRID_W), BF16)
    mask = jnp.asarray(np.where(col_ok.T, 0.0, NEG_INF).reshape(1, GRID_W * GRID_W), F32)
    rows = n_layer * H_B * n_dr
    rows_pad = -(-rows // 8) * 8
    rpb2 = jnp.zeros((rows_pad, 32), F32).at[:rows, :n_dc].set(na_rpb.reshape(rows, n_dc))
    tab = pl.pallas_call(
        _rpb_table_kernel,
        out_shape=jax.ShapeDtypeStruct((rows_pad, GRID_W * GRID_W), F32),
        compiler_params=pltpu.CompilerParams(vmem_limit_bytes=VMEM_LIMIT),
        name="rpb_table",
    )(rpb2, onehot, mask)
    tab = tab[:rows].reshape(n_layer, H_B, n_dr, GRID_W, GRID_W)
    zero = jnp.zeros_like(tab[:, :, :1])
    padded = jnp.concatenate([zero, tab, zero], axis=2)
    return jnp.concatenate([padded[:, :, 1:], padded[:, :, :-1]], axis=-1)


def _natten_kernel(h_ref, hc_ref, wqt_ref, wk_ref, wvt_ref, wg_ref, gqt_ref, gk_ref, bd_ref, tab_ref,
                   o_ref, k_scr, vt_scr, vtc_scr, *, n_lat, n_ctx):
    t = pl.program_id(1)
    n_rows = n_lat // GRID_W
    tq = NA_TILE * GRID_W
    band = NA_BAND * GRID_W

    @pl.when(t == 0)
    def _():
        _stage_k(h_ref, n_lat, 0, wk_ref, gk_ref[...], bd_ref[...], k_scr, None)
        _stage_k(hc_ref, n_ctx, n_lat, wk_ref, gk_ref[...], bd_ref[...], k_scr, None)
        for c in range(n_lat // NA_VBLK):
            vt_scr[c] = _dot_nt(wvt_ref[...], h_ref[0, c * NA_VBLK:(c + 1) * NA_VBLK, :]).astype(BF16)
        _stage_vt(hc_ref, n_ctx, 0, wvt_ref, vtc_scr)

    blk0 = jnp.clip(t * (NA_TILE // NA_VROWS) - 1, 0, n_rows // NA_VROWS - NA_BAND // NA_VROWS)
    u0 = blk0 * NA_VROWS
    k0 = pl.multiple_of(u0 * GRID_W, NA_VBLK)

    lane_row = lax.broadcasted_iota(jnp.int32, (1, tq), 1) >> GRID_SHIFT
    mask_rows, bias_idx = [], []
    for u in range(NA_BAND):
        row = jnp.zeros((1, tq), F32)
        for i in range(NA_TILE):
            r_band = jnp.clip(t * NA_TILE + i - NA_ROWS // 2, 0, n_rows - NA_ROWS)
            ok = (u0 + u >= r_band) & (u0 + u < r_band + NA_ROWS)
            row = jnp.where(lane_row == i, jnp.where(ok, 0.0, NEG_INF), row)
        mask_rows.append(row)
        j = u0 + u - t * NA_TILE + (NA_ROWS - 1)
        bias_idx.append([jnp.clip(j - 2 * ip, 0, 2 * NA_ROWS - 1) for ip in range(NA_TILE // 2)])

    q0 = pl.multiple_of(t * tq, tq)
    hh = h_ref[0, pl.ds(q0, tq), :]
    qh = _q_heads_t(hh, wqt_ref, gqt_ref, None)
    gate = _dot(hh, wg_ref[...])

    def scores(hd):
        return (_dot(k_scr[hd, pl.ds(k0, band), :], qh[hd]),
                _dot(k_scr[hd, n_lat:n_lat + n_ctx, :], qh[hd]))

    blocks = []
    ahead = scores(0)
    for hd in range(H_B):
        st, sc = ahead
        if hd + 1 < H_B:
            ahead = scores(hd + 1)
        rows = []
        for u in range(NA_BAND):
            bias = jnp.concatenate([tab_ref[hd, jj] for jj in bias_idx[u]], axis=1)
            rows.append(st[u * GRID_W:(u + 1) * GRID_W, :] + bias + mask_rows[u])
        st = jnp.concatenate(rows, axis=0)
        m = jnp.maximum(jnp.max(st, axis=0, keepdims=True), jnp.max(sc, axis=0, keepdims=True))
        p = jnp.exp2(st - m)
        pc = jnp.exp2(sc - m)
        l = jnp.sum(p, axis=0, keepdims=True) + jnp.sum(pc, axis=0, keepdims=True)
        pb = p.astype(BF16)
        hs = slice(hd * HEAD_DIM, (hd + 1) * HEAD_DIM)
        ot = _dot(vtc_scr[hs, :], pc.astype(BF16))
        for c in range(NA_BAND // NA_VROWS):
            ot = ot + _dot(vt_scr[blk0 + c, hs, :], pb[c * NA_VBLK:(c + 1) * NA_VBLK, :])
        blocks.append(ot / l)
    yt = jnp.concatenate(blocks, axis=0)
    o_ref[0] = (yt.T * _silu(gate)).astype(BF16)


def _natten(h, hc, wq, wk, wv, wg, gq, gk, bd, tab):
    nb, n_lat, d = h.shape
    n_ctx = hc.shape[1]
    n_rows = n_lat // GRID_W
    assert n_rows % NA_TILE == 0 and n_rows >= NA_BAND
    tq = NA_TILE * GRID_W
    full = lambda b, t: (b, 0, 0)
    gqt = jnp.broadcast_to(gq[:, None], (HEAD_DIM, LANES))
    consts = [wq.T, wk, wv.T, wg, gqt, jnp.tile(gk, H_B).reshape(1, W_B), bd, tab]
    return pl.pallas_call(
        functools.partial(_natten_kernel, n_lat=n_lat, n_ctx=n_ctx),
        grid=(nb, n_lat // tq),
        in_specs=[pl.BlockSpec((1, n_lat, d), full), pl.BlockSpec((1, n_ctx, d), full)]
                 + [_const_spec(a.shape) for a in consts],
        out_specs=pl.BlockSpec((1, tq, W_B), lambda b, t: (b, t, 0)),
        out_shape=jax.ShapeDtypeStruct((nb, n_lat, W_B), BF16),
        scratch_shapes=[pltpu.VMEM((H_B, n_lat + n_ctx, HEAD_DIM), BF16),
                        pltpu.VMEM((n_lat // NA_VBLK, W_B, NA_VBLK), BF16),
                        pltpu.VMEM((W_B, n_ctx), BF16)],
        compiler_params=_cparams("arbitrary", "arbitrary"),
        name="natten",
    )(h, hc, *consts)


def _merge_kernel(h_ref, ya_ref, yb_ref, yc_ref, yd_ref, wl_ref, bm_ref, wbr_ref, wo_ref, x_ref, mod_ref, o_ref):
    d = x_ref.shape[2]
    hh = h_ref[0]
    m = None
    for br, y_ref in enumerate((ya_ref, yb_ref, yc_ref, yd_ref)):
        cols = slice(br * d, (br + 1) * d)
        g = _sigmoid(_dot(hh, wl_ref[:, cols]) + bm_ref[:, cols])
        t = g * _dot(y_ref[0], wbr_ref[br])
        m = t if m is None else m + t
    gate = mod_ref[0, :, 2 * d:3 * d]
    o_ref[0] = x_ref[0] + gate * _dot(m.astype(BF16), wo_ref[...])


def _merge(h, ys, wl, b_merge, wbr, wo, x, mod, per_batch):
    nb, seq, d = x.shape
    tm = min(seq, 512)
    tile = lambda b, i: (b, i, 0)
    mod_map = (lambda b, i: (b, 0, 0)) if per_batch else (lambda b, i: (0, 0, 0))
    return pl.pallas_call(
        _merge_kernel,
        grid=(nb, seq // tm),
        in_specs=[pl.BlockSpec((1, tm, d), tile)]
                 + [pl.BlockSpec((1, tm, W_A), tile)] * N_BRANCH
                 + [_const_spec(wl.shape), _const_spec((1, N_BRANCH * d)), _const_spec(wbr.shape), _const_spec(wo.shape),
                    pl.BlockSpec((1, tm, d), tile),
                    pl.BlockSpec((1, 1, 3 * d), mod_map)],
        out_specs=pl.BlockSpec((1, tm, d), tile),
        out_shape=jax.ShapeDtypeStruct((nb, seq, d), F32),
        compiler_params=_cparams("arbitrary", "arbitrary"),
        name="merge",
    )(h, *ys, wl, b_merge.reshape(1, N_BRANCH * d), wbr, wo, x, mod)


def _rope_tables(seq):
    half = HEAD_DIM // 2
    t = np.arange(seq)
    freqs = ROPE_THETA ** (-np.arange(0, half, 2, dtype=np.float64) / half)
    blocks_c, blocks_s = [], []
    for pos in (t // GRID_W, t % GRID_W):
        ang = pos[:, None].astype(np.float64) * freqs[None, :]
        blocks_c += [np.cos(ang), np.cos(ang)]
        blocks_s += [-np.sin(ang), np.sin(ang)]
    cos = np.concatenate(blocks_c, axis=1)
    sin = np.concatenate(blocks_s, axis=1)
    reps = LANES // HEAD_DIM
    key_side = (jnp.asarray(np.tile(cos, (1, reps)), F32), jnp.asarray(np.tile(sin, (1, reps)), F32))
    query_side = (jnp.asarray(cos.T, F32), jnp.asarray(sin.T, F32))
    return key_side, query_side


def _group_mean_matrix():
    idx = np.arange(LANES) // HEAD_DIM
    return jnp.asarray((idx[:, None] == idx[None, :]) / HEAD_DIM, BF16)


def _qkvg(w, off, qw, kw, vw, gw):
    edges = np.cumsum([off, qw, kw, vw, gw])
    return tuple(w[:, a:b] for a, b in zip(edges[:-1], edges[1:]))


def kernel(x, c, ctx, c_ctx, w_ada, b_ada, norm_g, w_in, b_merge, conv_w, conv_b, conv_ln_g, conv_ln_b, na_qn_g, na_kn_g, na_rpb, gqa_qn_g, gqa_kn_g, diff_qn_g, diff_kn_g, lam_q1, lam_k1, lam_q2, lam_k2, diff_subln_g, w_br_a, w_br_b, w_br_c, w_br_d, w_out):
    n_batch, seq, d = x.shape
    depth = w_in.shape[0]
    rope = _rope_tables(seq)
    bd = _group_mean_matrix()
    tabs = _rpb_tables(na_rpb)

    rows = -(-(n_batch + 1) // 8) * 8
    cc = jnp.zeros((rows, d), F32).at[:n_batch].set(c).at[n_batch].set(c_ctx)
    mod_all = _modulation(cc, w_ada, b_ada)

    xc = ctx
    for l in range(depth):
        need_ctx = l < depth - 1
        lam_init = 0.8 - 0.6 * math.exp(-0.3 * l)
        mod_x = mod_all[l, :n_batch, None, :]
        mod_c = mod_all[l, n_batch:n_batch + 1, None, :]
        h = _hnorm(x, mod_x, norm_g[l], True)
        hc = _hnorm(xc, mod_c, norm_g[l], False)
        w = w_in[l].astype(BF16)
        w_a = w[:, OFF_A:OFF_B]
        w_b = _qkvg(w, OFF_B, W_B, W_B, W_B, W_B)
        w_c = _qkvg(w, OFF_C, W_C, W_C_KV, W_C_KV, W_C)
        w_d = _qkvg(w, OFF_D, W_D, W_D, W_D, W_D)
        w_l = w[:, OFF_L:]
        wbr = jnp.stack([w_br_a[l], w_br_b[l], w_br_c[l], w_br_d[l]]).astype(BF16)
        wo = w_out[l].astype(BF16)
        lam_p = jnp.stack([lam_q1[l], lam_k1[l], lam_q2[l], lam_k2[l]])

        ya = _conv_mixer(h, w_a, conv_w[l], conv_b[l], conv_ln_g[l], conv_ln_b[l])
        yb = _natten(h, hc, *w_b, na_qn_g[l], na_kn_g[l], bd, tabs[l])
        yc = _gqa(h, hc, *w_c, gqa_qn_g[l], gqa_kn_g[l], bd, rope, H_C, KV_C, 256)
        yd = _diff_attn(h, hc, *w_d, diff_qn_g[l], diff_kn_g[l], bd, lam_p, diff_subln_g[l], rope, lam_init, 512)
        x_new = _merge(h, (ya, yb, yc, yd), w_l, b_merge[l], wbr, wo, x, mod_x, True)

        if need_ctx:
            ca = _conv_mixer(hc, w_a, conv_w[l], conv_b[l], conv_ln_g[l], conv_ln_b[l])
            cb = _gqa(None, hc, *w_b, na_qn_g[l], na_kn_g[l], bd, None, H_B, H_B, 256)
            cg = _gqa(None, hc, *w_c, gqa_qn_g[l], gqa_kn_g[l], bd, None, H_C, KV_C, 256)
            cd = _diff_attn(None, hc, *w_d, diff_qn_g[l], diff_kn_g[l], bd, lam_p, diff_subln_g[l], None, lam_init, 256)
            xc = _merge(hc, (ca, cb, cg, cd), w_l, b_merge[l], wbr, wo, xc, mod_c, False)
        x = x_new
    return x
```
